```python
import jax, jax.numpy as jnp
from jax import lax
import numpy as np

D_MODEL = 1024
BATCH = 4
SEQ = 4096
DEPTH = 4

M_HEADS = 4
M_DQK = 128
M_DV = 256
M_QK = M_HEADS * M_DQK
M_V = M_HEADS * M_DV
M_CHUNK = 128
CONV_W = 4
G_GROUPS = 4
G_WIDTH = 1024
G_DG = G_WIDTH // G_GROUPS
G_CHUNK = 128
D_FF = 2816
EPS = 1e-6
W_IN_COLS = 2 * M_QK + M_V + 2 * M_HEADS + M_V + 2 * G_WIDTH + 2 * D_MODEL

kernel_name = "hybrid_mlstm_gmlp_macaron_sandwich"


def _split_points():
    sizes = (M_QK, M_QK, M_V, M_HEADS, M_HEADS, M_V, G_WIDTH, G_WIDTH, D_MODEL, D_MODEL)
    return [int(s) for s in np.cumsum(np.array(sizes))[:-1]]


def rmsnorm(x, g):
    xf = x.astype(jnp.float32)
    y = xf * lax.rsqrt(jnp.mean(xf * xf, axis=-1, keepdims=True) + EPS)
    return (y * g.astype(jnp.float32)).astype(x.dtype)


def layernorm(x, g, b=None):
    xf = x.astype(jnp.float32)
    mu = jnp.mean(xf, axis=-1, keepdims=True)
    var = jnp.mean(jnp.square(xf - mu), axis=-1, keepdims=True)
    y = (xf - mu) * lax.rsqrt(var + EPS) * g.astype(jnp.float32)
    if b is not None:
        y = y + b.astype(jnp.float32)
    return y.astype(x.dtype)


def swiglu(x, w_in, w_out):
    a, g = jnp.split(x @ w_in, 2, axis=-1)
    return (jax.nn.silu(a) * g) @ w_out


def causal_dwconv(x, w, b):
    S = x.shape[1]
    xp = jnp.pad(x, ((0, 0), (CONV_W - 1, 0), (0, 0)))
    y = b
    for j in range(CONV_W):
        y = y + w[j] * xp[:, j:j + S]
    return y


def mlstm_chunkwise(q, k, v, i_pre, f_pre):
    B, S, H, _ = q.shape
    L = M_CHUNK
    nc = S // L
    f32 = jnp.float32
    q = q.astype(f32) * (M_DQK ** -0.5)
    k = k.astype(f32)
    v = v.astype(f32)
    log_i = i_pre.astype(f32)
    log_f = jax.nn.log_sigmoid(f_pre.astype(f32))

    def heads_chunks(a):
        return a.reshape(B, nc, L, H, a.shape[-1]).transpose(1, 0, 3, 2, 4)

    def gate_chunks(a):
        return a.reshape(B, nc, L, H).transpose(1, 0, 3, 2)

    qc, kc, vc = heads_chunks(q), heads_chunks(k), heads_chunks(v)
    ic = gate_chunks(log_i)
    bc = jnp.cumsum(gate_chunks(log_f), axis=-1)
    causal = jnp.tril(jnp.ones((L, L), dtype=bool))

    def step(carry, xs):
        C, n, m = carry
        qx, kx, vx, ix, bx = xs
        D = bx[..., :, None] - bx[..., None, :] + ix[..., None, :]
        D = jnp.where(causal, D, -jnp.inf)
        inter = bx + m[..., None]
        m_t = jnp.maximum(inter, jnp.max(D, axis=-1))
        W = jnp.exp(D - m_t[..., None]) * jnp.einsum('bhtd,bhsd->bhts', qx, kx)
        s_inter = jnp.exp(inter - m_t)
        num = (jnp.einsum('bhts,bhsv->bhtv', W, vx)
               + s_inter[..., None] * jnp.einsum('bhvd,bhtd->bhtv', C, qx))
        den = jnp.sum(W, axis=-1) + s_inter * jnp.einsum('bhd,bhtd->bht', n, qx)
        h = num / jnp.maximum(jnp.abs(den), jnp.exp(-m_t))[..., None]
        bL = bx[..., -1]
        g = bL[..., None] - bx + ix
        m_new = jnp.maximum(bL + m, jnp.max(g, axis=-1))
        w_s = jnp.exp(g - m_new[..., None])
        decay = jnp.exp(bL + m - m_new)
        C_new = decay[..., None, None] * C + jnp.einsum('bhs,bhsv,bhsd->bhvd', w_s, vx, kx)
        n_new = decay[..., None] * n + jnp.einsum('bhs,bhsd->bhd', w_s, kx)
        return (C_new, n_new, m_new), h

    init = (jnp.zeros((B, H, M_DV, M_DQK), f32),
            jnp.zeros((B, H, M_DQK), f32),
            jnp.zeros((B, H), f32))
    _, h = lax.scan(step, init, (qc, kc, vc, ic, bc))
    return h.transpose(1, 0, 3, 2, 4).reshape(B, S, H, M_DV)


def chunked_spatial_gating(u, v, g_norm_g, g_norm_b, w_s, b_s):
    B, S, _ = v.shape
    L = G_CHUNK
    nc = S // L
    v = layernorm(v, g_norm_g, g_norm_b)
    vc = v.reshape(B, nc, L, G_GROUPS, G_DG)
    ws = jnp.where(jnp.tril(jnp.ones((L, L), dtype=bool)), w_s, 0.0)
    s = jnp.einsum('gts,bcsgd->bctgd', ws, vc) + b_s.T[None, None, :, :, None]
    return u * s.reshape(B, S, G_WIDTH)


def token_mixer(h, w_in, conv_w, conv_b, i_bias, f_bias, m_norm_g,
                g_norm_g, g_norm_b, w_s, b_s, p_a, p_b, w_out):
    B, S, _ = h.shape
    z = h @ w_in
    q_r, k_r, v_m, i_pre, f_pre, o_pre, u_g, v_g, gate_a, gate_b = jnp.split(
        z, _split_points(), axis=-1)
    qk = jax.nn.silu(causal_dwconv(jnp.concatenate([q_r, k_r], axis=-1), conv_w, conv_b))
    q, k = jnp.split(qk, 2, axis=-1)
    hm = mlstm_chunkwise(q.reshape(B, S, M_HEADS, M_DQK),
                         k.reshape(B, S, M_HEADS, M_DQK),
                         v_m.reshape(B, S, M_HEADS, M_DV),
                         i_pre + i_bias, f_pre + f_bias).astype(h.dtype)
    hm = layernorm(hm, m_norm_g.reshape(M_HEADS, M_DV)).reshape(B, S, M_V)
    y_a = (jax.nn.sigmoid(o_pre) * hm) @ p_a
    hg = chunked_spatial_gating(jax.nn.gelu(u_g), jax.nn.gelu(v_g),
                                g_norm_g, g_norm_b, w_s, b_s)
    y_b = hg @ p_b
    merged = jax.nn.sigmoid(gate_a) * y_a + jax.nn.sigmoid(gate_b) * y_b
    return merged @ w_out


def setup_inputs(seed: int = 0) -> dict:
    key = jax.random.key(seed)
    ks = jax.random.split(key, 24)
    f32 = jnp.float32

    def nrm(k, shape, fan_in, scale=1.0):
        return (jax.random.normal(k, shape, f32) * (scale * fan_in ** -0.5)).astype(f32)

    x = jax.random.normal(ks[0], (BATCH, SEQ, D_MODEL), f32)
    norm_gains = 1.0 + 0.05 * jax.random.normal(ks[1], (DEPTH, 6, D_MODEL), f32)
    ffn1_w_in = nrm(ks[2], (DEPTH, D_MODEL, 2 * D_FF), D_MODEL)
    ffn1_w_out = nrm(ks[3], (DEPTH, D_FF, D_MODEL), D_FF)
    w_in = nrm(ks[4], (DEPTH, D_MODEL, W_IN_COLS), D_MODEL)
    conv_w = nrm(ks[5], (DEPTH, CONV_W, 2 * M_QK), CONV_W)
    conv_b = 0.02 * jax.random.normal(ks[6], (DEPTH, 2 * M_QK), f32)
    i_bias = 0.1 * jax.random.normal(ks[7], (DEPTH, M_HEADS), f32)
    f_bias = (jnp.linspace(3.0, 6.0, M_HEADS, dtype=f32)[None, :]
              + 0.1 * jax.random.normal(ks[8], (DEPTH, M_HEADS), f32))
    m_norm_g = 1.0 + 0.05 * jax.random.normal(ks[9], (DEPTH, M_V), f32)
    g_norm_g = 1.0 + 0.05 * jax.random.normal(ks[10], (DEPTH, G_WIDTH), f32)
    g_norm_b = 0.02 * jax.random.normal(ks[11], (DEPTH, G_WIDTH), f32)
    w_s = nrm(ks[12], (DEPTH, G_GROUPS, G_CHUNK, G_CHUNK), G_CHUNK, 0.5)
    b_s = 1.0 + 0.02 * jax.random.normal(ks[13], (DEPTH, G_GROUPS, G_CHUNK), f32)
    p_a = nrm(ks[14], (DEPTH, M_V, D_MODEL), M_V)
    p_b = nrm(ks[15], (DEPTH, G_WIDTH, D_MODEL), G_WIDTH)
    w_out = nrm(ks[16], (DEPTH, D_MODEL, D_MODEL), D_MODEL)
    ffn2_w_in = nrm(ks[17], (DEPTH, D_MODEL, 2 * D_FF), D_MODEL)
    ffn2_w_out = nrm(ks[18], (DEPTH, D_FF, D_MODEL), D_FF)
    return {"x": x, "norm_gains": norm_gains, "ffn1_w_in": ffn1_w_in, "ffn1_w_out": ffn1_w_out,
            "w_in": w_in, "conv_w": conv_w, "conv_b": conv_b, "i_bias": i_bias, "f_bias": f_bias,
            "m_norm_g": m_norm_g, "g_norm_g": g_norm_g, "g_norm_b": g_norm_b, "w_s": w_s,
            "b_s": b_s, "p_a": p_a, "p_b": p_b, "w_out": w_out,
            "ffn2_w_in": ffn2_w_in, "ffn2_w_out": ffn2_w_out}


def reference(x, norm_gains, ffn1_w_in, ffn1_w_out, w_in, conv_w, conv_b, i_bias, f_bias,
              m_norm_g, g_norm_g, g_norm_b, w_s, b_s, p_a, p_b, w_out,
              ffn2_w_in, ffn2_w_out):
    for l in range(DEPTH):
        g = norm_gains[l]
        x = x + 0.5 * rmsnorm(swiglu(rmsnorm(x, g[0]), ffn1_w_in[l], ffn1_w_out[l]), g[1])
        mix = token_mixer(rmsnorm(x, g[2]), w_in[l], conv_w[l], conv_b[l], i_bias[l], f_bias[l],
                          m_norm_g[l], g_norm_g[l], g_norm_b[l], w_s[l], b_s[l],
                          p_a[l], p_b[l], w_out[l])
        x = x + rmsnorm(mix, g[3])
        x = x + 0.5 * rmsnorm(swiglu(rmsnorm(x, g[4]), ffn2_w_in[l], ffn2_w_out[l]), g[5])
    return x
```

```python
import functools

import jax
import jax.numpy as jnp
from jax import lax
from jax.experimental import pallas as pl
from jax.experimental.pallas import tpu as pltpu

D_MODEL = 1024
DEPTH = 4
M_HEADS = 4
M_DQK = 128
M_DV = 256
M_QK = M_HEADS * M_DQK
M_V = M_HEADS * M_DV
CHUNK = 128
CONV_W = 4
G_GROUPS = 4
G_WIDTH = 1024
G_DG = G_WIDTH // G_GROUPS
D_FF = 2816
EPS = 1e-6
GATE_LANES = 128
CONV_HALO = 8

FFN_TM = 512
MIX_TM = 256
VMEM_LIMIT_BYTES = 56 * 1024 * 1024

BF16 = jnp.bfloat16
F32 = jnp.float32


def _dot(a, b):
    return jnp.dot(a, b, preferred_element_type=F32)


def _rms(x, g):
    return x * lax.rsqrt(jnp.mean(x * x, axis=-1, keepdims=True) + EPS) * g


def _ln(x, g):
    mu = jnp.mean(x, axis=-1, keepdims=True)
    xc = x - mu
    var = jnp.mean(xc * xc, axis=-1, keepdims=True)
    return xc * lax.rsqrt(var + EPS) * g


def _log_sigmoid(x):
    return jnp.minimum(x, 0.0) - jnp.log1p(jnp.exp(-jnp.abs(x)))


def _ffn_kernel(x_ref, gains_ref, w_in_ref, w_out_ref, o_ref, *, gi, go):
    x = x_ref[...]
    h = _rms(x, gains_ref[gi:gi + 1, :]).astype(BF16)
    a = _dot(h, w_in_ref[:, :D_FF])
    g = _dot(h, w_in_ref[:, D_FF:])
    s = (jax.nn.silu(a) * g).astype(BF16)
    y = _dot(s, w_out_ref[...])
    o_ref[...] = x + 0.5 * _rms(y, gains_ref[go:go + 1, :])


def _resident(shape, index_map):
    return pl.BlockSpec(shape, index_map, pipeline_mode=pl.Buffered(1))


def _ffn(x2d, gains, w_in, w_out, layer, gi, go):
    t = x2d.shape[0]
    return pl.pallas_call(
        functools.partial(_ffn_kernel, gi=gi, go=go),
        grid=(t // FFN_TM,),
        in_specs=[
            pl.BlockSpec((FFN_TM, D_MODEL), lambda i: (i, 0)),
            _resident((None, 6, D_MODEL), lambda i: (layer, 0, 0)),
            _resident((None, D_MODEL, 2 * D_FF), lambda i: (layer, 0, 0)),
            _resident((None, D_FF, D_MODEL), lambda i: (layer, 0, 0)),
        ],
        out_specs=pl.BlockSpec((FFN_TM, D_MODEL), lambda i: (i, 0)),
        out_shape=jax.ShapeDtypeStruct(x2d.shape, F32),
        compiler_params=pltpu.CompilerParams(
            dimension_semantics=("arbitrary",), vmem_limit_bytes=VMEM_LIMIT_BYTES),
        name="ffn",
    )(x2d, gains, w_in, w_out)


def _chunk_cumsum(v, row_in_chunk):
    shift = 1
    while shift < CHUNK:
        rolled = pltpu.roll(v, shift, axis=0)
        v = v + jnp.where(row_in_chunk >= shift, rolled, 0.0)
        shift *= 2
    return v


def _mixer_kernel(x_ref, gains_ref, wqk_ref, wv_ref, wif_ref, wo_ref, wu_ref, wvg_ref,
                  wga_ref, wgb_ref, convw_ref, convb_ref, ifb_ref, mng_ref, gng_ref, gnb_ref,
                  ws_ref, bst_ref, pa_ref, pb_ref, wout_ref, o_ref,
                  qkbuf, ct_ref, n_ref, m_ref, ya_ref, hg_ref):
    tm = MIX_TM
    nch = tm // CHUNK

    @pl.when(pl.program_id(1) == 0)
    def _():
        qkbuf[0:CONV_HALO, :] = jnp.zeros((CONV_HALO, 2 * M_QK), F32)
        ct_ref[...] = jnp.zeros_like(ct_ref)
        n_ref[...] = jnp.zeros_like(n_ref)
        m_ref[...] = jnp.zeros_like(m_ref)

    x = x_ref[...]
    h = _rms(x, gains_ref[2:3, :]).astype(BF16)

    qkbuf[CONV_HALO:CONV_HALO + tm, :] = _dot(h, wqk_ref[...])
    conv = convb_ref[...]
    for j in range(CONV_W):
        off = CONV_HALO - (CONV_W - 1 - j)
        conv = conv + convw_ref[j:j + 1, :] * qkbuf[off:off + tm, :]
    qkbuf[0:CONV_HALO, :] = qkbuf[tm:tm + CONV_HALO, :]
    qk = jax.nn.silu(conv)
    q_all = qk[:, :M_QK] * (M_DQK ** -0.5)
    k_all = qk[:, M_QK:]
    v_all = _dot(h, wv_ref[...])
    o_gate = jax.nn.sigmoid(_dot(h, wo_ref[...]))

    pre = _dot(h, wif_ref[...]) + ifb_ref[...]
    col = lax.broadcasted_iota(jnp.int32, (tm, GATE_LANES), 1)
    row_in_chunk = lax.broadcasted_iota(jnp.int32, (tm, GATE_LANES), 0) % CHUNK
    gates = jnp.where(col >= M_HEADS, _chunk_cumsum(_log_sigmoid(pre), row_in_chunk), pre)

    tri_r = lax.broadcasted_iota(jnp.int32, (CHUNK, CHUNK), 0)
    tri_c = lax.broadcasted_iota(jnp.int32, (CHUNK, CHUNK), 1)
    causal = tri_r >= tri_c

    for c in range(nch):
        rows = slice(c * CHUNK, (c + 1) * CHUNK)
        gc = gates[rows, :]
        gct = gc.T
        for hd in range(M_HEADS):
            q = q_all[rows, hd * M_DQK:(hd + 1) * M_DQK]
            k = k_all[rows, hd * M_DQK:(hd + 1) * M_DQK]
            v = v_all[rows, hd * M_DV:(hd + 1) * M_DV].astype(BF16)
            i_col = gc[:, hd:hd + 1]
            b_col = gc[:, M_HEADS + hd:M_HEADS + hd + 1]
            i_row = gct[hd:hd + 1, :]
            b_row = gct[M_HEADS + hd:M_HEADS + hd + 1, :]
            b_last = b_row[:, CHUNK - 1:CHUNK]
            m_prev = m_ref[hd][:, 0:1]
            ct = ct_ref[hd]
            n_row = n_ref[hd]

            a_row = i_row - b_row
            d = jnp.where(causal, b_col + a_row, -jnp.inf)
            inter = b_col + m_prev
            m_t = jnp.maximum(inter, jnp.max(d, axis=-1, keepdims=True))
            s = lax.dot_general(q.astype(BF16), k.astype(BF16), (((1,), (1,)), ((), ())),
                                preferred_element_type=F32)
            p = jnp.exp(d - m_t) * s
            s_inter = jnp.exp(inter - m_t)
            lhs = jnp.concatenate([p, s_inter * q], axis=1).astype(BF16)
            rhs = jnp.concatenate([v, ct.astype(BF16)], axis=0)
            num = _dot(lhs, rhs)
            den = (jnp.sum(p, axis=-1, keepdims=True)
                   + s_inter * jnp.sum(q * n_row, axis=-1, keepdims=True))
            hh = num / jnp.maximum(jnp.abs(den), jnp.exp(-m_t))

            g_row = b_last + a_row
            m_new = jnp.maximum(b_last + m_prev, jnp.max(g_row, axis=-1, keepdims=True))
            w_col = jnp.exp(b_last - b_col + i_col - m_new)
            decay = jnp.exp(b_last + m_prev - m_new)
            kw = w_col * k
            ct_ref[hd] = decay * ct + lax.dot_general(
                kw.astype(BF16), v, (((0,), (0,)), ((), ())), preferred_element_type=F32)
            n_ref[hd] = decay * n_row + jnp.sum(kw, axis=0, keepdims=True)
            m_ref[hd] = jnp.broadcast_to(m_new, (1, GATE_LANES))

            hm = _ln(hh, mng_ref[:, hd * M_DV:(hd + 1) * M_DV])
            ya_ref[rows, hd * M_DV:(hd + 1) * M_DV] = (
                o_gate[rows, hd * M_DV:(hd + 1) * M_DV] * hm).astype(BF16)

    y_a = _dot(ya_ref[...], pa_ref[...])

    u = jax.nn.gelu(_dot(h, wu_ref[...]))
    vg = jax.nn.gelu(_dot(h, wvg_ref[...]))
    vln = (_ln(vg, gng_ref[...]) + gnb_ref[...]).astype(BF16)
    for grp in range(G_GROUPS):
        w_tri = jnp.where(causal, ws_ref[grp], 0.0).astype(BF16)
        bias_col = bst_ref[:, grp:grp + 1]
        for c in range(nch):
            rows = slice(c * CHUNK, (c + 1) * CHUNK)
            cols = slice(grp * G_DG, (grp + 1) * G_DG)
            sg = _dot(w_tri, vln[rows, cols]) + bias_col
            hg_ref[rows, cols] = (u[rows, cols] * sg).astype(BF16)
    y_b = _dot(hg_ref[...], pb_ref[...])

    merged = (jax.nn.sigmoid(_dot(h, wga_ref[...])) * y_a
              + jax.nn.sigmoid(_dot(h, wgb_ref[...])) * y_b)
    mix = _dot(merged.astype(BF16), wout_ref[...])
    o_ref[...] = x + _rms(mix, gains_ref[3:4, :])


def _mixer(x3d, gains, w, layer):
    b, s, _ = x3d.shape
    tm = MIX_TM

    def lay(shape):
        nd = len(shape)
        return _resident((None,) + shape, lambda bi, j: (layer,) + (0,) * nd)

    sq = (D_MODEL, D_MODEL)
    in_specs = [
        pl.BlockSpec((None, tm, D_MODEL), lambda bi, j: (bi, j, 0)),
        lay((6, D_MODEL)),
        lay(sq), lay(sq), lay((D_MODEL, GATE_LANES)), lay(sq), lay(sq), lay(sq), lay(sq), lay(sq),
        lay((CONV_W, 2 * M_QK)), lay((1, 2 * M_QK)), lay((1, GATE_LANES)),
        lay((1, M_V)), lay((1, G_WIDTH)), lay((1, G_WIDTH)),
        lay((G_GROUPS, CHUNK, CHUNK)), lay((CHUNK, G_GROUPS)),
        lay(sq), lay(sq), lay(sq),
    ]
    return pl.pallas_call(
        _mixer_kernel,
        grid=(b, s // tm),
        in_specs=in_specs,
        out_specs=pl.BlockSpec((None, tm, D_MODEL), lambda bi, j: (bi, j, 0)),
        out_shape=jax.ShapeDtypeStruct(x3d.shape, F32),
        scratch_shapes=[
            pltpu.VMEM((tm + CONV_HALO, 2 * M_QK), F32),
            pltpu.VMEM((M_HEADS, M_DQK, M_DV), F32),
            pltpu.VMEM((M_HEADS, 1, M_DQK), F32),
            pltpu.VMEM((M_HEADS, 1, GATE_LANES), F32),
            pltpu.VMEM((tm, M_V), BF16),
            pltpu.VMEM((tm, G_WIDTH), BF16),
        ],
        compiler_params=pltpu.CompilerParams(
            dimension_semantics=("arbitrary", "arbitrary"), vmem_limit_bytes=VMEM_LIMIT_BYTES),
        name="mixer",
    )(x3d, gains, w["qk"], w["v"], w["if"], w["o"], w["u"], w["vg"], w["ga"], w["gb"],
      w["conv_w"], w["conv_b"], w["if_bias"], w["m_norm_g"], w["g_norm_g"], w["g_norm_b"],
      w["w_s"], w["b_s_t"], w["p_a"], w["p_b"], w["w_out"])


def _split_w_in(w_in):
    sizes = (M_QK, M_QK, M_V, M_HEADS, M_HEADS, M_V, G_WIDTH, G_WIDTH, D_MODEL, D_MODEL)
    offs = [0]
    for sz in sizes:
        offs.append(offs[-1] + sz)
    wb = w_in.astype(BF16)
    cut = lambda a, z: wb[:, :, offs[a]:offs[z]]
    w_if = jnp.pad(cut(3, 5), ((0, 0), (0, 0), (0, GATE_LANES - 2 * M_HEADS)))
    return {"qk": cut(0, 2), "v": cut(2, 3), "if": w_if, "o": cut(5, 6), "u": cut(6, 7),
            "vg": cut(7, 8), "ga": cut(8, 9), "gb": cut(9, 10)}


def kernel(x, norm_gains, ffn1_w_in, ffn1_w_out, w_in, conv_w, conv_b, i_bias, f_bias, m_norm_g, g_norm_g, g_norm_b, w_s, b_s, p_a, p_b, w_out, ffn2_w_in, ffn2_w_out):
    b, s, d = x.shape
    w = _split_w_in(w_in)
    w.update(
        conv_w=conv_w,
        conv_b=conv_b[:, None, :],
        if_bias=jnp.pad(jnp.concatenate([i_bias, f_bias], axis=-1),
                        ((0, 0), (0, GATE_LANES - 2 * M_HEADS)))[:, None, :],
        m_norm_g=m_norm_g[:, None, :],
        g_norm_g=g_norm_g[:, None, :],
        g_norm_b=g_norm_b[:, None, :],
        w_s=w_s,
        b_s_t=jnp.swapaxes(b_s, 1, 2),
        p_a=p_a.astype(BF16), p_b=p_b.astype(BF16), w_out=w_out.astype(BF16),
    )
    f1_in, f1_out = ffn1_w_in.astype(BF16), ffn1_w_out.astype(BF16)
    f2_in, f2_out = ffn2_w_in.astype(BF16), ffn2_w_out.astype(BF16)

    for layer in range(DEPTH):
        x2d = _ffn(x.reshape(b * s, d), norm_gains, f1_in, f1_out, layer, 0, 1)
        x = _mixer(x2d.reshape(b, s, d), norm_gains, w, layer)
        x = _ffn(x.reshape(b * s, d), norm_gains, f2_in, f2_out, layer, 4, 5).reshape(b, s, d)
    return x
```

```python
import functools

import jax
import jax.numpy as jnp
from jax import lax
from jax.experimental import pallas as pl
from jax.experimental.pallas import tpu as pltpu

D_MODEL = 1024
DEPTH = 4
M_HEADS = 4
M_DQK = 128
M_DV = 256
M_QK = M_HEADS * M_DQK
M_V = M_HEADS * M_DV
CHUNK = 128
CONV_W = 4
G_GROUPS = 4
G_WIDTH = 1024
G_DG = G_WIDTH // G_GROUPS
D_FF = 2816
EPS = 1e-6
GATE_LANES = 128
SUBLANES = 8

FFN_TM = 512
MIX_TM = 256
VMEM_LIMIT_BYTES = 56 * 1024 * 1024

BF16 = jnp.bfloat16
F32 = jnp.float32


def _dot(a, b):
    return jnp.dot(a, b, preferred_element_type=F32)


def _rms(x, g):
    return x * lax.rsqrt(jnp.mean(x * x, axis=-1, keepdims=True) + EPS) * g


def _ln(x, g):
    mu = jnp.mean(x, axis=-1, keepdims=True)
    xc = x - mu
    var = jnp.mean(xc * xc, axis=-1, keepdims=True)
    return xc * lax.rsqrt(var + EPS) * g


def _log_sigmoid(x):
    return jnp.minimum(x, 0.0) - jnp.log1p(jnp.exp(-jnp.abs(x)))


def _ffn_kernel(x_ref, gains_ref, w_in_ref, w_out_ref, o_ref, *, gi, go):
    x = x_ref[...]
    h = _rms(x, gains_ref[gi:gi + 1, :]).astype(BF16)
    a = _dot(h, w_in_ref[:, :D_FF])
    g = _dot(h, w_in_ref[:, D_FF:])
    s = (jax.nn.silu(a) * g).astype(BF16)
    y = _dot(s, w_out_ref[:, :D_MODEL])
    o_ref[...] = x + 0.5 * _rms(y, gains_ref[go:go + 1, :])


def _resident(shape, index_map):
    return pl.BlockSpec(shape, index_map, pipeline_mode=pl.Buffered(1))


def _ffn(x2d, gains, w_in, w_out, layer, gi, go):
    t = x2d.shape[0]
    return pl.pallas_call(
        functools.partial(_ffn_kernel, gi=gi, go=go),
        grid=(t // FFN_TM,),
        in_specs=[
            pl.BlockSpec((FFN_TM, D_MODEL), lambda i: (i, 0)),
            _resident((None, 6, D_MODEL), lambda i: (layer, 0, 0)),
            _resident((None, D_MODEL, 2 * D_FF), lambda i: (layer, 0, 0)),
            _resident((None, D_FF, FFN_OUT_WIDTH), lambda i: (layer, 0, 0)),
        ],
        out_specs=pl.BlockSpec((FFN_TM, D_MODEL), lambda i: (i, 0)),
        out_shape=jax.ShapeDtypeStruct(x2d.shape, F32),
        compiler_params=pltpu.CompilerParams(
            dimension_semantics=("arbitrary",), vmem_limit_bytes=VMEM_LIMIT_BYTES),
        name="ffn",
    )(x2d, gains, w_in, w_out)


_ACT_SHAPES = (
    ("q", M_QK, F32), ("k", M_QK, F32), ("v", M_V, BF16), ("gates", GATE_LANES, F32),
    ("og", M_V, F32), ("u", G_WIDTH, F32), ("vln", G_WIDTH, BF16),
    ("sga", D_MODEL, F32), ("sgb", D_MODEL, F32),
)
_N_ACT = len(_ACT_SHAPES)
_MIX_WEIGHTS = ("h", "p", "conv_w", "conv_b", "if_bias",
                "m_norm_g", "g_norm_g", "g_norm_b", "w_s", "b_s_t")

LANES = 128
PAD_COLS = LANES


def _col_ranges(names_widths):
    out, off = {}, 0
    for name, width in names_widths:
        out[name] = (off, off + width)
        off += width
    return out, off


_H_COLS, _H_WIDTH = _col_ranges((("qk", 2 * M_QK), ("v", M_V), ("if", GATE_LANES), ("o", M_V),
                                 ("u", G_WIDTH), ("vg", G_WIDTH), ("ga", D_MODEL), ("gb", D_MODEL)))
_P_COLS, _P_WIDTH = _col_ranges((("p_a", D_MODEL), ("p_b", D_MODEL), ("w_out", D_MODEL),
                                 ("pad", PAD_COLS)))
FFN_OUT_WIDTH = D_MODEL + PAD_COLS


def _chunk_cumsum(v, row_in_chunk):
    shift = 1
    while shift < CHUNK:
        rolled = pltpu.roll(v, shift, axis=0)
        v = v + jnp.where(row_in_chunk >= shift, rolled, 0.0)
        shift *= 2
    return v


def _shift_rows(cur, tail, s, row_in_vreg):
    rolled = pltpu.roll(cur, s, axis=0)
    head = jnp.where(row_in_vreg < s, pltpu.roll(tail, s, axis=0), rolled[0:SUBLANES])
    return jnp.concatenate([head, rolled[SUBLANES:]], axis=0)


def _mixer_step(xf_ref, xb_ref, gains_ref, w, o_ref, act_w, act_r, tail_ref, ct_ref, n_ref,
                m_ref, ya_ref, hg_ref):
    tm = MIX_TM
    nch = tm // CHUNK

    tri_r = lax.broadcasted_iota(jnp.int32, (CHUNK, CHUNK), 0)
    tri_c = lax.broadcasted_iota(jnp.int32, (CHUNK, CHUNK), 1)
    causal = tri_r >= tri_c

    def mlstm_unit(c, hd):
        rows = slice(c * CHUNK, (c + 1) * CHUNK)
        qk_cols = slice(hd * M_DQK, (hd + 1) * M_DQK)
        v_cols = slice(hd * M_DV, (hd + 1) * M_DV)
        gc = act_r["gates"][rows, :]
        gct = gc.T
        q = act_r["q"][rows, qk_cols]
        k = act_r["k"][rows, qk_cols]
        v = act_r["v"][rows, v_cols]
        i_col = gc[:, hd:hd + 1]
        b_col = gc[:, M_HEADS + hd:M_HEADS + hd + 1]
        i_row = gct[hd:hd + 1, :]
        b_row = gct[M_HEADS + hd:M_HEADS + hd + 1, :]
        b_last = b_row[:, CHUNK - 1:CHUNK]
        m_prev = m_ref[hd][:, 0:1]
        ct = ct_ref[hd]
        n_row = n_ref[hd]

        a_row = i_row - b_row
        d = jnp.where(causal, b_col + a_row, -jnp.inf)
        inter = b_col + m_prev
        m_t = jnp.maximum(inter, jnp.max(d, axis=-1, keepdims=True))
        s = lax.dot_general(q.astype(BF16), k.astype(BF16), (((1,), (1,)), ((), ())),
                            preferred_element_type=F32)
        p = jnp.exp(d - m_t) * s
        s_inter = jnp.exp(inter - m_t)
        lhs = jnp.concatenate([p, s_inter * q], axis=1).astype(BF16)
        rhs = jnp.concatenate([v, ct.astype(BF16)], axis=0)
        num = _dot(lhs, rhs)
        den = (jnp.sum(p, axis=-1, keepdims=True)
               + s_inter * jnp.sum(q * n_row, axis=-1, keepdims=True))
        hh = num / jnp.maximum(jnp.abs(den), jnp.exp(-m_t))

        g_row = b_last + a_row
        m_new = jnp.maximum(b_last + m_prev, jnp.max(g_row, axis=-1, keepdims=True))
        w_col = jnp.exp(b_last - b_col + i_col - m_new)
        decay = jnp.exp(b_last + m_prev - m_new)
        kw = w_col * k
        ct_ref[hd] = decay * ct + lax.dot_general(
            kw.astype(BF16), v, (((0,), (0,)), ((), ())), preferred_element_type=F32)
        n_ref[hd] = decay * n_row + jnp.sum(kw, axis=0, keepdims=True)
        m_ref[hd] = jnp.broadcast_to(m_new, (1, GATE_LANES))

        hm = _ln(hh, w["m_norm_g"][:, v_cols])
        ya_ref[rows, v_cols] = (act_r["og"][rows, v_cols] * hm).astype(BF16)

    def gmlp_group(grp):
        w_tri = jnp.where(causal, w["w_s"][grp], 0.0).astype(BF16)
        bias_col = w["b_s_t"][:, grp:grp + 1]
        cols = slice(grp * G_DG, (grp + 1) * G_DG)
        for c in range(nch):
            rows = slice(c * CHUNK, (c + 1) * CHUNK)
            sg = _dot(w_tri, act_r["vln"][rows, cols]) + bias_col
            hg_ref[rows, cols] = (act_r["u"][rows, cols] * sg).astype(BF16)

    def wh(name):
        lo, hi = _H_COLS[name]
        return w["h"][:, lo:hi]

    def wp(name):
        lo, hi = _P_COLS[name]
        return w["p"][:, lo:hi]

    x = xf_ref[...]
    h = _rms(x, gains_ref[2:3, :]).astype(BF16)

    zqk = _dot(h, wh("qk"))
    mlstm_unit(0, 0)

    pre = _dot(h, wh("if")) + w["if_bias"][...]
    col = lax.broadcasted_iota(jnp.int32, (tm, GATE_LANES), 1)
    row_in_chunk = lax.broadcasted_iota(jnp.int32, (tm, GATE_LANES), 0) % CHUNK
    act_w["gates"][...] = jnp.where(
        col >= M_HEADS, _chunk_cumsum(_log_sigmoid(pre), row_in_chunk), pre)
    mlstm_unit(0, 1)

    act_w["v"][...] = _dot(h, wh("v")).astype(BF16)

    row_in_vreg = lax.broadcasted_iota(jnp.int32, (SUBLANES, 2 * M_QK), 0)
    tail = tail_ref[...]
    conv = w["conv_b"][...] + w["conv_w"][CONV_W - 1:CONV_W, :] * zqk
    for s in range(1, CONV_W):
        j = CONV_W - 1 - s
        conv = conv + w["conv_w"][j:j + 1, :] * _shift_rows(zqk, tail, s, row_in_vreg)
    tail_ref[...] = zqk[tm - SUBLANES:tm, :]
    qk = jax.nn.silu(conv)
    act_w["q"][...] = qk[:, :M_QK] * (M_DQK ** -0.5)
    act_w["k"][...] = qk[:, M_QK:]
    gmlp_group(0)
    gmlp_group(1)

    act_w["og"][...] = jax.nn.sigmoid(_dot(h, wh("o")))
    mlstm_unit(0, 2)

    act_w["u"][...] = jax.nn.gelu(_dot(h, wh("u")))
    mlstm_unit(0, 3)
    gmlp_group(2)
    gmlp_group(3)

    vg = jax.nn.gelu(_dot(h, wh("vg")))
    mlstm_unit(1, 0)
    y_b = _dot(hg_ref[...], wp("p_b"))
    act_w["vln"][...] = (_ln(vg, w["g_norm_g"][...]) + w["g_norm_b"][...]).astype(BF16)
    mlstm_unit(1, 1)

    act_w["sga"][...] = jax.nn.sigmoid(_dot(h, wh("ga")))
    mlstm_unit(1, 2)

    zgb = _dot(h, wh("gb"))
    mlstm_unit(1, 3)

    y_a = _dot(ya_ref[...], wp("p_a"))
    act_w["sgb"][...] = jax.nn.sigmoid(zgb)
    merged = act_r["sga"][...] * y_a + act_r["sgb"][...] * y_b
    mix = _dot(merged.astype(BF16), wp("w_out"))
    o_ref[...] = xb_ref[...] + _rms(mix, gains_ref[3:4, :])


def _mixer_kernel(*refs, tiles_per_row):
    xf_ref, xb_ref, gains_ref = refs[:3]
    nw = len(_MIX_WEIGHTS)
    w = dict(zip(_MIX_WEIGHTS, refs[3:3 + nw]))
    o_ref = refs[3 + nw]
    scratch = refs[4 + nw:]
    act = [dict(zip((n for n, _, _ in _ACT_SHAPES), scratch[i * _N_ACT:(i + 1) * _N_ACT]))
           for i in range(2)]
    tail_ref, ct_ref, n_ref, m_ref, ya_ref, hg_ref = scratch[2 * _N_ACT:]
    t = pl.program_id(0)

    @pl.when(t == 0)
    def _():
        for ref in act[1].values():
            ref[...] = jnp.zeros_like(ref)

    @pl.when(t % tiles_per_row == 0)
    def _():
        tail_ref[...] = jnp.zeros_like(tail_ref)

    @pl.when(jnp.logical_or(t == 0, (t + tiles_per_row - 1) % tiles_per_row == 0))
    def _():
        ct_ref[...] = jnp.zeros_like(ct_ref)
        n_ref[...] = jnp.zeros_like(n_ref)
        m_ref[...] = jnp.zeros_like(m_ref)

    step = functools.partial(_mixer_step, xf_ref, xb_ref, gains_ref, w, o_ref)
    rest = (tail_ref, ct_ref, n_ref, m_ref, ya_ref, hg_ref)

    @pl.when(t % 2 == 0)
    def _():
        step(act[0], act[1], *rest)

    @pl.when(t % 2 == 1)
    def _():
        step(act[1], act[0], *rest)


def _mixer(x2d, gains, w, layer, tiles_per_row):
    tm = MIX_TM
    n_tiles = x2d.shape[0] // tm

    def lay(shape):
        nd = len(shape)
        return _resident((None,) + shape, lambda t: (layer,) + (0,) * nd)

    w_specs = {
        "h": lay((D_MODEL, _H_WIDTH)), "p": lay((D_MODEL, _P_WIDTH)),
        "conv_w": lay((CONV_W, 2 * M_QK)), "conv_b": lay((1, 2 * M_QK)),
        "if_bias": lay((1, GATE_LANES)), "m_norm_g": lay((1, M_V)),
        "g_norm_g": lay((1, G_WIDTH)), "g_norm_b": lay((1, G_WIDTH)),
        "w_s": lay((G_GROUPS, CHUNK, CHUNK)), "b_s_t": lay((CHUNK, G_GROUPS)),
    }
    in_specs = [
        pl.BlockSpec((tm, D_MODEL), lambda t: (jnp.minimum(t, n_tiles - 1), 0)),
        pl.BlockSpec((tm, D_MODEL), lambda t: (jnp.maximum(t - 1, 0), 0)),
        lay((6, D_MODEL)),
    ] + [w_specs[name] for name in _MIX_WEIGHTS]
    act_scratch = [pltpu.VMEM((tm, width), dt) for _, width, dt in _ACT_SHAPES]
    return pl.pallas_call(
        functools.partial(_mixer_kernel, tiles_per_row=tiles_per_row),
        grid=(n_tiles + 1,),
        in_specs=in_specs,
        out_specs=pl.BlockSpec((tm, D_MODEL), lambda t: (jnp.maximum(t - 1, 0), 0)),
        out_shape=jax.ShapeDtypeStruct(x2d.shape, F32),
        scratch_shapes=act_scratch + act_scratch + [
            pltpu.VMEM((SUBLANES, 2 * M_QK), F32),
            pltpu.VMEM((M_HEADS, M_DQK, M_DV), F32),
            pltpu.VMEM((M_HEADS, 1, M_DQK), F32),
            pltpu.VMEM((M_HEADS, 1, GATE_LANES), F32),
            pltpu.VMEM((tm, M_V), BF16),
            pltpu.VMEM((tm, G_WIDTH), BF16),
        ],
        compiler_params=pltpu.CompilerParams(
            dimension_semantics=("arbitrary",), vmem_limit_bytes=VMEM_LIMIT_BYTES),
        name="mixer",
    )(x2d, x2d, gains, *[w[name] for name in _MIX_WEIGHTS])


def _pack_w_in(w_in):
    gate_end = 2 * M_QK + M_V + 2 * M_HEADS
    wb = w_in.astype(BF16)
    pad = jnp.zeros(wb.shape[:2] + (GATE_LANES - 2 * M_HEADS,), BF16)
    return jnp.concatenate([wb[:, :, :gate_end], pad, wb[:, :, gate_end:]], axis=-1)


def _pad_cols(w):
    return jnp.pad(w.astype(BF16), ((0, 0), (0, 0), (0, PAD_COLS)))


def kernel(x, norm_gains, ffn1_w_in, ffn1_w_out, w_in, conv_w, conv_b, i_bias, f_bias, m_norm_g, g_norm_g, g_norm_b, w_s, b_s, p_a, p_b, w_out, ffn2_w_in, ffn2_w_out):
    b, s, d = x.shape
    w = dict(
        h=_pack_w_in(w_in),
        p=_pad_cols(jnp.concatenate([p_a, p_b, w_out], axis=-1)),
        conv_w=conv_w,
        conv_b=conv_b[:, None, :],
        if_bias=jnp.pad(jnp.concatenate([i_bias, f_bias], axis=-1),
                        ((0, 0), (0, GATE_LANES - 2 * M_HEADS)))[:, None, :],
        m_norm_g=m_norm_g[:, None, :],
        g_norm_g=g_norm_g[:, None, :],
        g_norm_b=g_norm_b[:, None, :],
        w_s=w_s,
        b_s_t=jnp.swapaxes(b_s, 1, 2),
    )
    f1_in, f1_out = ffn1_w_in.astype(BF16), _pad_cols(ffn1_w_out)
    f2_in, f2_out = ffn2_w_in.astype(BF16), _pad_cols(ffn2_w_out)

    x2d = x.reshape(b * s, d)
    for layer in range(DEPTH):
        x2d = _ffn(x2d, norm_gains, f1_in, f1_out, layer, 0, 1)
        x2d = _mixer(x2d, norm_gains, w, layer, s // MIX_TM)
        x2d = _ffn(x2d, norm_gains, f2_in, f2_out, layer, 4, 5)
    return x2d.reshape(b, s, d)
```

```python
import functools

import jax
import jax.numpy as jnp
from jax import lax
from jax.experimental import pallas as pl
from jax.experimental.pallas import tpu as pltpu

D_MODEL = 1024
DEPTH = 4
M_HEADS = 4
M_DQK = 128
M_DV = 256
M_QK = M_HEADS * M_DQK
M_V = M_HEADS * M_DV
CHUNK = 128
CONV_W = 4
G_GROUPS = 4
G_WIDTH = 1024
G_DG = G_WIDTH // G_GROUPS
D_FF = 2816
EPS = 1e-6
GATE_LANES = 128
SUBLANES = 8
MXU_COLS = 256

FFN_TM = 512
MIX_TM = 256
VMEM_LIMIT_BYTES = 56 * 1024 * 1024

BF16 = jnp.bfloat16
F32 = jnp.float32


def _dot(a, b):
    return jnp.dot(a, b, preferred_element_type=F32)


def _rms(x, g):
    return x * lax.rsqrt(jnp.mean(x * x, axis=-1, keepdims=True) + EPS) * g


def _ln(x, g):
    mu = jnp.mean(x, axis=-1, keepdims=True)
    xc = x - mu
    var = jnp.mean(xc * xc, axis=-1, keepdims=True)
    return xc * lax.rsqrt(var + EPS) * g


def _log_sigmoid(x):
    return jnp.minimum(x, 0.0) - jnp.log1p(jnp.exp(-jnp.abs(x)))


def _ffn_kernel(x_ref, gains_ref, w_in_ref, w_out_ref, o_ref, *, gi, go):
    x = x_ref[...]
    h = _rms(x, gains_ref[gi:gi + 1, :]).astype(BF16)
    a = _dot(h, w_in_ref[:, :D_FF])
    g = _dot(h, w_in_ref[:, D_FF:])
    s = (jax.nn.silu(a) * g).astype(BF16)
    y = _dot(s, w_out_ref[:, :D_MODEL])
    o_ref[...] = x + 0.5 * _rms(y, gains_ref[go:go + 1, :])


def _resident(shape, index_map):
    return pl.BlockSpec(shape, index_map, pipeline_mode=pl.Buffered(1))


def _ffn(x2d, gains, w_in, w_out, layer, gi, go):
    t = x2d.shape[0]
    return pl.pallas_call(
        functools.partial(_ffn_kernel, gi=gi, go=go),
        grid=(t // FFN_TM,),
        in_specs=[
            pl.BlockSpec((FFN_TM, D_MODEL), lambda i: (i, 0)),
            _resident((None, 6, D_MODEL), lambda i: (layer, 0, 0)),
            _resident((None, D_MODEL, 2 * D_FF), lambda i: (layer, 0, 0)),
            _resident((None, D_FF, FFN_OUT_WIDTH), lambda i: (layer, 0, 0)),
        ],
        out_specs=pl.BlockSpec((FFN_TM, D_MODEL), lambda i: (i, 0)),
        out_shape=jax.ShapeDtypeStruct(x2d.shape, F32),
        compiler_params=pltpu.CompilerParams(
            dimension_semantics=("arbitrary",), vmem_limit_bytes=VMEM_LIMIT_BYTES),
        name="ffn",
    )(x2d, gains, w_in, w_out)


_ACT_SHAPES = (
    ("qk", 2 * M_QK, F32), ("if", GATE_LANES, F32), ("v", M_V, BF16), ("o", M_V, F32),
    ("u", G_WIDTH, F32), ("vg", G_WIDTH, F32), ("ga", D_MODEL, F32), ("gb", D_MODEL, F32),
)
_N_ACT = len(_ACT_SHAPES)
_MIX_WEIGHTS = ("h", "p", "conv_w", "conv_b", "if_bias",
                "m_norm_g", "g_norm_g", "g_norm_b", "w_s", "b_s_t")

LANES = 128
PAD_COLS = LANES


def _col_ranges(names_widths):
    out, off = {}, 0
    for name, width in names_widths:
        out[name] = (off, off + width)
        off += width
    return out, off


_H_COLS, _H_WIDTH = _col_ranges((("qk", 2 * M_QK), ("v", M_V), ("if", GATE_LANES), ("o", M_V),
                                 ("u", G_WIDTH), ("vg", G_WIDTH), ("ga", D_MODEL), ("gb", D_MODEL)))
_P_COLS, _P_WIDTH = _col_ranges((("p_a", D_MODEL), ("p_b", D_MODEL), ("w_out", D_MODEL),
                                 ("pad", PAD_COLS)))
FFN_OUT_WIDTH = D_MODEL + PAD_COLS


def _chunk_cumsum(v, row_in_chunk):
    shift = 1
    while shift < CHUNK:
        rolled = pltpu.roll(v, shift, axis=0)
        v = v + jnp.where(row_in_chunk >= shift, rolled, 0.0)
        shift *= 2
    return v


def _shift_rows(cur, tail, s, row_in_vreg):
    rolled = pltpu.roll(cur, s, axis=0)
    head = jnp.where(row_in_vreg < s, pltpu.roll(tail, s, axis=0), rolled[0:SUBLANES])
    return jnp.concatenate([head, rolled[SUBLANES:]], axis=0)


def _mixer_step(xf_ref, xb_ref, gains_ref, w, o_ref, act_w, act_r, tail_ref, ct_ref, n_ref,
                m_ref, ya_ref, hg_ref, q_ref, k_ref, gates_ref, og_ref, u_ref, vln_ref):
    tm = MIX_TM
    nch = tm // CHUNK

    tri_r = lax.broadcasted_iota(jnp.int32, (CHUNK, CHUNK), 0)
    tri_c = lax.broadcasted_iota(jnp.int32, (CHUNK, CHUNK), 1)
    causal = tri_r >= tri_c

    def mlstm_unit(c, hd):
        rows = slice(c * CHUNK, (c + 1) * CHUNK)
        qk_cols = slice(hd * M_DQK, (hd + 1) * M_DQK)
        v_cols = slice(hd * M_DV, (hd + 1) * M_DV)
        gc = gates_ref[rows, :]
        gct = gc.T
        q = q_ref[rows, qk_cols]
        k = k_ref[rows, qk_cols]
        v = act_r["v"][rows, v_cols]
        i_col = gc[:, hd:hd + 1]
        b_col = gc[:, M_HEADS + hd:M_HEADS + hd + 1]
        i_row = gct[hd:hd + 1, :]
        b_row = gct[M_HEADS + hd:M_HEADS + hd + 1, :]
        b_last = b_row[:, CHUNK - 1:CHUNK]
        m_prev = m_ref[hd][:, 0:1]
        ct = ct_ref[hd]
        n_row = n_ref[hd]

        a_row = i_row - b_row
        d = jnp.where(causal, b_col + a_row, -jnp.inf)
        inter = b_col + m_prev
        m_t = jnp.maximum(inter, jnp.max(d, axis=-1, keepdims=True))
        s = lax.dot_general(q.astype(BF16), k.astype(BF16), (((1,), (1,)), ((), ())),
                            preferred_element_type=F32)
        p = jnp.exp(d - m_t) * s
        s_inter = jnp.exp(inter - m_t)
        lhs = jnp.concatenate([p, s_inter * q], axis=1).astype(BF16)
        rhs = jnp.concatenate([v, ct.astype(BF16)], axis=0)
        num = _dot(lhs, rhs)
        den = (jnp.sum(p, axis=-1, keepdims=True)
               + s_inter * jnp.sum(q * n_row, axis=-1, keepdims=True))
        hh = num / jnp.maximum(jnp.abs(den), jnp.exp(-m_t))

        g_row = b_last + a_row
        m_new = jnp.maximum(b_last + m_prev, jnp.max(g_row, axis=-1, keepdims=True))
        w_col = jnp.exp(b_last - b_col + i_col - m_new)
        decay = jnp.exp(b_last + m_prev - m_new)
        kw = w_col * k
        ct_ref[hd] = decay * ct + lax.dot_general(
            kw.astype(BF16), v, (((0,), (0,)), ((), ())), preferred_element_type=F32)
        n_ref[hd] = decay * n_row + jnp.sum(kw, axis=0, keepdims=True)
        m_ref[hd] = jnp.broadcast_to(m_new, (1, GATE_LANES))

        hm = _ln(hh, w["m_norm_g"][:, v_cols])
        ya_ref[rows, v_cols] = (og_ref[rows, v_cols] * hm).astype(BF16)

    def gmlp_group(grp):
        w_tri = jnp.where(causal, w["w_s"][grp], 0.0).astype(BF16)
        bias_col = w["b_s_t"][:, grp:grp + 1]
        cols = slice(grp * G_DG, (grp + 1) * G_DG)
        u_ref[:, cols] = jax.nn.gelu(act_r["u"][:, cols])
        for c in range(nch):
            rows = slice(c * CHUNK, (c + 1) * CHUNK)
            sg = _dot(w_tri, vln_ref[rows, cols]) + bias_col
            hg_ref[rows, cols] = (u_ref[rows, cols] * sg).astype(BF16)

    def wp(name):
        lo, hi = _P_COLS[name]
        return w["p"][:, lo:hi]

    x = xf_ref[...]
    h = _rms(x, gains_ref[2:3, :]).astype(BF16)

    def proj_block(name, lo, hi):
        base = _H_COLS[name][0]
        z = _dot(h, w["h"][:, base + lo:base + hi])
        act_w[name][:, lo:hi] = z.astype(act_w[name].dtype)

    def gates():
        pre = act_r["if"][...] + w["if_bias"][...]
        col = lax.broadcasted_iota(jnp.int32, (tm, GATE_LANES), 1)
        row_in_chunk = lax.broadcasted_iota(jnp.int32, (tm, GATE_LANES), 0) % CHUNK
        gates_ref[...] = jnp.where(
            col >= M_HEADS, _chunk_cumsum(_log_sigmoid(pre), row_in_chunk), pre)

    def conv_block(lo, hi):
        zqk = act_r["qk"][:, lo:hi]
        row_in_vreg = lax.broadcasted_iota(jnp.int32, (SUBLANES, hi - lo), 0)
        tail = tail_ref[:, lo:hi]
        conv = w["conv_b"][:, lo:hi] + w["conv_w"][CONV_W - 1:CONV_W, lo:hi] * zqk
        for s in range(1, CONV_W):
            j = CONV_W - 1 - s
            conv = conv + w["conv_w"][j:j + 1, lo:hi] * _shift_rows(zqk, tail, s, row_in_vreg)
        tail_ref[:, lo:hi] = zqk[tm - SUBLANES:tm, :]
        y = jax.nn.silu(conv)
        if lo < M_QK:
            q_ref[:, lo:hi] = y * (M_DQK ** -0.5)
        else:
            k_ref[:, lo - M_QK:hi - M_QK] = y

    def out_gate(hd):
        v_cols = slice(hd * M_DV, (hd + 1) * M_DV)
        og_ref[:, v_cols] = jax.nn.sigmoid(act_r["o"][:, v_cols])

    def gmlp_norm():
        vln_ref[...] = (_ln(jax.nn.gelu(act_r["vg"][...]), w["g_norm_g"][...])
                        + w["g_norm_b"][...]).astype(BF16)

    proj = []
    for name, width, _ in _ACT_SHAPES:
        for lo in range(0, width, MXU_COLS):
            hi = min(lo + MXU_COLS, width)
            proj.append((hi - lo, functools.partial(proj_block, name, lo, hi)))
    rec = [(300, gates)]
    rec += [(350, functools.partial(conv_block, lo, lo + MXU_COLS))
            for lo in range(0, 2 * M_QK, MXU_COLS)]
    rec += [(100, functools.partial(out_gate, hd)) for hd in range(M_HEADS)]
    rec += [(450, functools.partial(mlstm_unit, 0, hd)) for hd in range(M_HEADS)]
    rec += [(900, gmlp_norm)]
    rec += [(300, functools.partial(gmlp_group, grp)) for grp in range(G_GROUPS)]
    rec += [(450, functools.partial(mlstm_unit, c, hd))
            for c in range(1, nch) for hd in range(M_HEADS)]
    proj_total = sum(c for c, _ in proj)
    rec_total = sum(c for c, _ in rec)
    done_proj, done_rec, ip = 0, 0, 0
    for cost, run in rec:
        while ip < len(proj) and done_proj * rec_total <= done_rec * proj_total:
            done_proj += proj[ip][0]
            proj[ip][1]()
            ip += 1
        run()
        done_rec += cost
    for _, run in proj[ip:]:
        run()

    y_b = _dot(hg_ref[...], wp("p_b"))
    y_a = _dot(ya_ref[...], wp("p_a"))
    merged = jax.nn.sigmoid(act_r["ga"][...]) * y_a + jax.nn.sigmoid(act_r["gb"][...]) * y_b
    mix = _dot(merged.astype(BF16), wp("w_out"))
    o_ref[...] = xb_ref[...] + _rms(mix, gains_ref[3:4, :])


def _mixer_kernel(*refs, tiles_per_row):
    xf_ref, xb_ref, gains_ref = refs[:3]
    nw = len(_MIX_WEIGHTS)
    w = dict(zip(_MIX_WEIGHTS, refs[3:3 + nw]))
    o_ref = refs[3 + nw]
    scratch = refs[4 + nw:]
    act = [dict(zip((n for n, _, _ in _ACT_SHAPES), scratch[i * _N_ACT:(i + 1) * _N_ACT]))
           for i in range(2)]
    rest = scratch[2 * _N_ACT:]
    tail_ref, ct_ref, n_ref, m_ref = rest[:4]
    t = pl.program_id(0)

    @pl.when(t == 0)
    def _():
        for ref in act[1].values():
            ref[...] = jnp.zeros_like(ref)

    @pl.when(jnp.logical_or(t == 0, (t + tiles_per_row - 1) % tiles_per_row == 0))
    def _():
        tail_ref[...] = jnp.zeros_like(tail_ref)
        ct_ref[...] = jnp.zeros_like(ct_ref)
        n_ref[...] = jnp.zeros_like(n_ref)
        m_ref[...] = jnp.zeros_like(m_ref)

    step = functools.partial(_mixer_step, xf_ref, xb_ref, gains_ref, w, o_ref)

    @pl.when(t % 2 == 0)
    def _():
        step(act[0], act[1], *rest)

    @pl.when(t % 2 == 1)
    def _():
        step(act[1], act[0], *rest)


def _mixer(x2d, gains, w, layer, tiles_per_row):
    tm = MIX_TM
    n_tiles = x2d.shape[0] // tm

    def lay(shape):
        nd = len(shape)
        return _resident((None,) + shape, lambda t: (layer,) + (0,) * nd)

    w_specs = {
        "h": lay((D_MODEL, _H_WIDTH)), "p": lay((D_MODEL, _P_WIDTH)),
        "conv_w": lay((CONV_W, 2 * M_QK)), "conv_b": lay((1, 2 * M_QK)),
        "if_bias": lay((1, GATE_LANES)), "m_norm_g": lay((1, M_V)),
        "g_norm_g": lay((1, G_WIDTH)), "g_norm_b": lay((1, G_WIDTH)),
        "w_s": lay((G_GROUPS, CHUNK, CHUNK)), "b_s_t": lay((CHUNK, G_GROUPS)),
    }
    in_specs = [
        pl.BlockSpec((tm, D_MODEL), lambda t: (jnp.minimum(t, n_tiles - 1), 0)),
        pl.BlockSpec((tm, D_MODEL), lambda t: (jnp.maximum(t - 1, 0), 0)),
        lay((6, D_MODEL)),
    ] + [w_specs[name] for name in _MIX_WEIGHTS]
    act_scratch = [pltpu.VMEM((tm, width), dt) for _, width, dt in _ACT_SHAPES]
    return pl.pallas_call(
        functools.partial(_mixer_kernel, tiles_per_row=tiles_per_row),
        grid=(n_tiles + 1,),
        in_specs=in_specs,
        out_specs=pl.BlockSpec((tm, D_MODEL), lambda t: (jnp.maximum(t - 1, 0), 0)),
        out_shape=jax.ShapeDtypeStruct(x2d.shape, F32),
        scratch_shapes=act_scratch + act_scratch + [
            pltpu.VMEM((SUBLANES, 2 * M_QK), F32),
            pltpu.VMEM((M_HEADS, M_DQK, M_DV), F32),
            pltpu.VMEM((M_HEADS, 1, M_DQK), F32),
            pltpu.VMEM((M_HEADS, 1, GATE_LANES), F32),
            pltpu.VMEM((tm, M_V), BF16),
            pltpu.VMEM((tm, G_WIDTH), BF16),
            pltpu.VMEM((tm, M_QK), F32),
            pltpu.VMEM((tm, M_QK), F32),
            pltpu.VMEM((tm, GATE_LANES), F32),
            pltpu.VMEM((tm, M_V), F32),
            pltpu.VMEM((tm, G_WIDTH), F32),
            pltpu.VMEM((tm, G_WIDTH), BF16),
        ],
        compiler_params=pltpu.CompilerParams(
            dimension_semantics=("arbitrary",), vmem_limit_bytes=VMEM_LIMIT_BYTES),
        name="mixer",
    )(x2d, x2d, gains, *[w[name] for name in _MIX_WEIGHTS])


def _pack_w_in(w_in):
    gate_end = 2 * M_QK + M_V + 2 * M_HEADS
    pad = jnp.zeros(w_in.shape[:2] + (GATE_LANES - 2 * M_HEADS,), w_in.dtype)
    return jnp.concatenate([w_in[:, :, :gate_end], pad, w_in[:, :, gate_end:]],
                           axis=-1).astype(BF16)


def _pad_cols(w):
    return jnp.pad(w, ((0, 0), (0, 0), (0, PAD_COLS))).astype(BF16)


def kernel(x, norm_gains, ffn1_w_in, ffn1_w_out, w_in, conv_w, conv_b, i_bias, f_bias, m_norm_g, g_norm_g, g_norm_b, w_s, b_s, p_a, p_b, w_out, ffn2_w_in, ffn2_w_out):
    b, s, d = x.shape
    w = dict(
        h=_pack_w_in(w_in),
        p=_pad_cols(jnp.concatenate([p_a, p_b, w_out], axis=-1)),
        conv_w=conv_w,
        conv_b=conv_b[:, None, :],
        if_bias=jnp.pad(jnp.concatenate([i_bias, f_bias], axis=-1),
                        ((0, 0), (0, GATE_LANES - 2 * M_HEADS)))[:, None, :],
        m_norm_g=m_norm_g[:, None, :],
        g_norm_g=g_norm_g[:, None, :],
        g_norm_b=g_norm_b[:, None, :],
        w_s=w_s,
        b_s_t=jnp.swapaxes(b_s, 1, 2),
    )
    f1_in, f1_out = ffn1_w_in.astype(BF16), _pad_cols(ffn1_w_out)
    f2_in, f2_out = ffn2_w_in.astype(BF16), _pad_cols(ffn2_w_out)

    x2d = x.reshape(b * s, d)
    for layer in range(DEPTH):
        x2d = _ffn(x2d, norm_gains, f1_in, f1_out, layer, 0, 1)
        x2d = _mixer(x2d, norm_gains, w, layer, s // MIX_TM)
        x2d = _ffn(x2d, norm_gains, f2_in, f2_out, layer, 4, 5)
    return x2d.reshape(b, s, d)
```

```python
import functools

import jax
import jax.numpy as jnp
from jax import lax
from jax.experimental import pallas as pl
from jax.experimental.pallas import tpu as pltpu

D_MODEL = 1024
DEPTH = 4
M_HEADS = 4
M_DQK = 128
M_DV = 256
M_QK = M_HEADS * M_DQK
M_V = M_HEADS * M_DV
CHUNK = 128
CONV_W = 4
G_GROUPS = 4
G_WIDTH = 1024
G_DG = G_WIDTH // G_GROUPS
D_FF = 2816
EPS = 1e-6
GATE_LANES = 128
SUBLANES = 8
MXU_COLS = 256

FFN_TM = 512
FFN_ROWS = 256
MIX_TM = 256
VMEM_LIMIT_BYTES = 56 * 1024 * 1024

BF16 = jnp.bfloat16
F32 = jnp.float32


def _dot(a, b):
    return jnp.dot(a, b, preferred_element_type=F32)


def _rms(x, g):
    return x * lax.rsqrt(jnp.mean(x * x, axis=-1, keepdims=True) + EPS) * g


def _ln(x, g):
    mu = jnp.mean(x, axis=-1, keepdims=True)
    xc = x - mu
    var = jnp.mean(xc * xc, axis=-1, keepdims=True)
    return xc * lax.rsqrt(var + EPS) * g


def _log_sigmoid(x):
    return jnp.minimum(x, 0.0) - jnp.log1p(jnp.exp(-jnp.abs(x)))


def _ffn_kernel(x_ref, gains_ref, w_in_ref, w_out_ref, o_ref, *, gi, go):
    for r in range(0, FFN_TM, FFN_ROWS):
        rows = slice(r, r + FFN_ROWS)
        x = x_ref[rows, :]
        h = _rms(x, gains_ref[gi:gi + 1, :]).astype(BF16)
        a = _dot(h, w_in_ref[:, :D_FF])
        g = _dot(h, w_in_ref[:, D_FF:])
        s = (jax.nn.silu(a) * g).astype(BF16)
        y = _dot(s, w_out_ref[:, :D_MODEL])
        o_ref[rows, :] = x + 0.5 * _rms(y, gains_ref[go:go + 1, :])


def _resident(shape, index_map):
    return pl.BlockSpec(shape, index_map, pipeline_mode=pl.Buffered(1))


def _ffn(x2d, gains, w_in, w_out, layer, gi, go):
    t = x2d.shape[0]
    return pl.pallas_call(
        functools.partial(_ffn_kernel, gi=gi, go=go),
        grid=(t // FFN_TM,),
        in_specs=[
            pl.BlockSpec((FFN_TM, D_MODEL), lambda i: (i, 0)),
            _resident((None, 6, D_MODEL), lambda i: (layer, 0, 0)),
            _resident((None, D_MODEL, 2 * D_FF), lambda i: (layer, 0, 0)),
            _resident((None, D_FF, FFN_OUT_WIDTH), lambda i: (layer, 0, 0)),
        ],
        out_specs=pl.BlockSpec((FFN_TM, D_MODEL), lambda i: (i, 0)),
        out_shape=jax.ShapeDtypeStruct(x2d.shape, F32),
        compiler_params=pltpu.CompilerParams(
            dimension_semantics=("arbitrary",), vmem_limit_bytes=VMEM_LIMIT_BYTES),
        name="ffn",
    )(x2d, gains, w_in, w_out)


_ACT_SHAPES = (
    ("qk", 2 * M_QK, F32), ("if", GATE_LANES, F32), ("v", M_V, BF16), ("o", M_V, F32),
    ("u", G_WIDTH, F32), ("vg", G_WIDTH, F32), ("ga", D_MODEL, F32), ("gb", D_MODEL, F32),
)
_N_ACT = len(_ACT_SHAPES)
_MIX_WEIGHTS = ("h", "p", "conv_w", "conv_b", "if_bias",
                "m_norm_g", "g_norm_g", "g_norm_b", "w_s", "b_s_t")

LANES = 128
PAD_COLS = LANES


def _col_ranges(names_widths):
    out, off = {}, 0
    for name, width in names_widths:
        out[name] = (off, off + width)
        off += width
    return out, off


_H_COLS, _H_WIDTH = _col_ranges((("qk", 2 * M_QK), ("v", M_V), ("if", GATE_LANES), ("o", M_V),
                                 ("u", G_WIDTH), ("vg", G_WIDTH), ("ga", D_MODEL), ("gb", D_MODEL)))
_P_COLS, _P_WIDTH = _col_ranges((("p_a", D_MODEL), ("p_b", D_MODEL), ("w_out", D_MODEL),
                                 ("pad", PAD_COLS)))
FFN_OUT_WIDTH = D_MODEL + PAD_COLS


def _chunk_cumsum(v, row_in_chunk):
    shift = 1
    while shift < CHUNK:
        rolled = pltpu.roll(v, shift, axis=0)
        v = v + jnp.where(row_in_chunk >= shift, rolled, 0.0)
        shift *= 2
    return v


def _shift_rows(cur, tail, s, row_in_vreg):
    rolled = pltpu.roll(cur, s, axis=0)
    head = jnp.where(row_in_vreg < s, pltpu.roll(tail, s, axis=0), rolled[0:SUBLANES])
    return jnp.concatenate([head, rolled[SUBLANES:]], axis=0)


def _mixer_step(xf_ref, xb_ref, gains_ref, w, o_ref, act_w, act_r, tail_ref, ct_ref, n_ref,
                m_ref, ya_ref, hg_ref, q_ref, k_ref, gates_ref, og_ref, u_ref, vln_ref):
    tm = MIX_TM
    nch = tm // CHUNK

    tri_r = lax.broadcasted_iota(jnp.int32, (CHUNK, CHUNK), 0)
    tri_c = lax.broadcasted_iota(jnp.int32, (CHUNK, CHUNK), 1)
    causal = tri_r >= tri_c

    def mlstm_unit(c, hd):
        rows = slice(c * CHUNK, (c + 1) * CHUNK)
        qk_cols = slice(hd * M_DQK, (hd + 1) * M_DQK)
        v_cols = slice(hd * M_DV, (hd + 1) * M_DV)
        gc = gates_ref[rows, :]
        gct = gc.T
        q = q_ref[rows, qk_cols]
        k = k_ref[rows, qk_cols]
        v = act_r["v"][rows, v_cols]
        i_col = gc[:, hd:hd + 1]
        b_col = gc[:, M_HEADS + hd:M_HEADS + hd + 1]
        i_row = gct[hd:hd + 1, :]
        b_row = gct[M_HEADS + hd:M_HEADS + hd + 1, :]
        b_last = b_row[:, CHUNK - 1:CHUNK]
        m_prev = m_ref[hd][:, 0:1]
        ct = ct_ref[hd]
        n_row = n_ref[hd]

        a_row = i_row - b_row
        d = jnp.where(causal, b_col + a_row, -jnp.inf)
        inter = b_col + m_prev
        m_t = jnp.maximum(inter, jnp.max(d, axis=-1, keepdims=True))
        s = lax.dot_general(q.astype(BF16), k.astype(BF16), (((1,), (1,)), ((), ())),
                            preferred_element_type=F32)
        p = jnp.exp(d - m_t) * s
        s_inter = jnp.exp(inter - m_t)
        lhs = jnp.concatenate([p, s_inter * q], axis=1).astype(BF16)
        rhs = jnp.concatenate([v, ct.astype(BF16)], axis=0)
        num = _dot(lhs, rhs)
        den = (jnp.sum(p, axis=-1, keepdims=True)
               + s_inter * jnp.sum(q * n_row, axis=-1, keepdims=True))
        hh = num / jnp.maximum(jnp.abs(den), jnp.exp(-m_t))

        g_row = b_last + a_row
        m_new = jnp.maximum(b_last + m_prev, jnp.max(g_row, axis=-1, keepdims=True))
        w_col = jnp.exp(b_last - b_col + i_col - m_new)
        decay = jnp.exp(b_last + m_prev - m_new)
        kw = w_col * k
        ct_ref[hd] = decay * ct + lax.dot_general(
            kw.astype(BF16), v, (((0,), (0,)), ((), ())), preferred_element_type=F32)
        n_ref[hd] = decay * n_row + jnp.sum(kw, axis=0, keepdims=True)
        m_ref[hd] = jnp.broadcast_to(m_new, (1, GATE_LANES))

        hm = _ln(hh, w["m_norm_g"][:, v_cols])
        ya_ref[rows, v_cols] = (og_ref[rows, v_cols] * hm).astype(BF16)

    def gmlp_group(grp):
        w_tri = jnp.where(causal, w["w_s"][grp], 0.0).astype(BF16)
        bias_col = w["b_s_t"][:, grp:grp + 1]
        cols = slice(grp * G_DG, (grp + 1) * G_DG)
        u_ref[:, cols] = jax.nn.gelu(act_r["u"][:, cols])
        for c in range(nch):
            rows = slice(c * CHUNK, (c + 1) * CHUNK)
            sg = _dot(w_tri, vln_ref[rows, cols]) + bias_col
            hg_ref[rows, cols] = (u_ref[rows, cols] * sg).astype(BF16)

    def wp(name):
        lo, hi = _P_COLS[name]
        return w["p"][:, lo:hi]

    x = xf_ref[...]
    h = _rms(x, gains_ref[2:3, :]).astype(BF16)

    def proj_block(name, lo, hi):
        base = _H_COLS[name][0]
        z = _dot(h, w["h"][:, base + lo:base + hi])
        act_w[name][:, lo:hi] = z.astype(act_w[name].dtype)

    def gates():
        pre = act_r["if"][...] + w["if_bias"][...]
        col = lax.broadcasted_iota(jnp.int32, (tm, GATE_LANES), 1)
        row_in_chunk = lax.broadcasted_iota(jnp.int32, (tm, GATE_LANES), 0) % CHUNK
        gates_ref[...] = jnp.where(
            col >= M_HEADS, _chunk_cumsum(_log_sigmoid(pre), row_in_chunk), pre)

    def conv_block(lo, hi):
        zqk = act_r["qk"][:, lo:hi]
        row_in_vreg = lax.broadcasted_iota(jnp.int32, (SUBLANES, hi - lo), 0)
        tail = tail_ref[:, lo:hi]
        conv = w["conv_b"][:, lo:hi] + w["conv_w"][CONV_W - 1:CONV_W, lo:hi] * zqk
        for s in range(1, CONV_W):
            j = CONV_W - 1 - s
            conv = conv + w["conv_w"][j:j + 1, lo:hi] * _shift_rows(zqk, tail, s, row_in_vreg)
        tail_ref[:, lo:hi] = zqk[tm - SUBLANES:tm, :]
        y = jax.nn.silu(conv)
        if lo < M_QK:
            q_ref[:, lo:hi] = y * (M_DQK ** -0.5)
        else:
            k_ref[:, lo - M_QK:hi - M_QK] = y

    def out_gate(hd):
        v_cols = slice(hd * M_DV, (hd + 1) * M_DV)
        og_ref[:, v_cols] = jax.nn.sigmoid(act_r["o"][:, v_cols])

    def gmlp_norm():
        vln_ref[...] = (_ln(jax.nn.gelu(act_r["vg"][...]), w["g_norm_g"][...])
                        + w["g_norm_b"][...]).astype(BF16)

    proj = []
    for name, width, _ in _ACT_SHAPES:
        for lo in range(0, width, MXU_COLS):
            hi = min(lo + MXU_COLS, width)
            proj.append((hi - lo, functools.partial(proj_block, name, lo, hi)))
    rec = [(300, gates)]
    rec += [(350, functools.partial(conv_block, lo, lo + MXU_COLS))
            for lo in range(0, 2 * M_QK, MXU_COLS)]
    rec += [(100, functools.partial(out_gate, hd)) for hd in range(M_HEADS)]
    rec += [(450, functools.partial(mlstm_unit, 0, hd)) for hd in range(M_HEADS)]
    rec += [(900, gmlp_norm)]
    rec += [(300, functools.partial(gmlp_group, grp)) for grp in range(G_GROUPS)]
    rec += [(450, functools.partial(mlstm_unit, c, hd))
            for c in range(1, nch) for hd in range(M_HEADS)]
    proj_total = sum(c for c, _ in proj)
    rec_total = sum(c for c, _ in rec)
    done_proj, done_rec, ip = 0, 0, 0
    for cost, run in rec:
        while ip < len(proj) and done_proj * rec_total <= done_rec * proj_total:
            done_proj += proj[ip][0]
            proj[ip][1]()
            ip += 1
        run()
        done_rec += cost
    for _, run in proj[ip:]:
        run()

    y_b = _dot(hg_ref[...], wp("p_b"))
    y_a = _dot(ya_ref[...], wp("p_a"))
    merged = jax.nn.sigmoid(act_r["ga"][...]) * y_a + jax.nn.sigmoid(act_r["gb"][...]) * y_b
    mix = _dot(merged.astype(BF16), wp("w_out"))
    o_ref[...] = xb_ref[...] + _rms(mix, gains_ref[3:4, :])


def _mixer_kernel(*refs, tiles_per_row):
    xf_ref, xb_ref, gains_ref = refs[:3]
    nw = len(_MIX_WEIGHTS)
    w = dict(zip(_MIX_WEIGHTS, refs[3:3 + nw]))
    o_ref = refs[3 + nw]
    scratch = refs[4 + nw:]
    act = [dict(zip((n for n, _, _ in _ACT_SHAPES), scratch[i * _N_ACT:(i + 1) * _N_ACT]))
           for i in range(2)]
    rest = scratch[2 * _N_ACT:]
    tail_ref, ct_ref, n_ref, m_ref = rest[:4]
    t = pl.program_id(0)

    @pl.when(t == 0)
    def _():
        for ref in act[1].values():
            ref[...] = jnp.zeros_like(ref)

    @pl.when(jnp.logical_or(t == 0, (t + tiles_per_row - 1) % tiles_per_row == 0))
    def _():
        tail_ref[...] = jnp.zeros_like(tail_ref)
        ct_ref[...] = jnp.zeros_like(ct_ref)
        n_ref[...] = jnp.zeros_like(n_ref)
        m_ref[...] = jnp.zeros_like(m_ref)

    step = functools.partial(_mixer_step, xf_ref, xb_ref, gains_ref, w, o_ref)

    @pl.when(t % 2 == 0)
    def _():
        step(act[0], act[1], *rest)

    @pl.when(t % 2 == 1)
    def _():
        step(act[1], act[0], *rest)


def _mixer(x2d, gains, w, layer, tiles_per_row):
    tm = MIX_TM
    n_tiles = x2d.shape[0] // tm

    def lay(shape):
        nd = len(shape)
        return _resident((None,) + shape, lambda t: (layer,) + (0,) * nd)

    w_specs = {
        "h": lay((D_MODEL, _H_WIDTH)), "p": lay((D_MODEL, _P_WIDTH)),
        "conv_w": lay((CONV_W, 2 * M_QK)), "conv_b": lay((1, 2 * M_QK)),
        "if_bias": lay((1, GATE_LANES)), "m_norm_g": lay((1, M_V)),
        "g_norm_g": lay((1, G_WIDTH)), "g_norm_b": lay((1, G_WIDTH)),
        "w_s": lay((G_GROUPS, CHUNK, CHUNK)), "b_s_t": lay((CHUNK, G_GROUPS)),
    }
    in_specs = [
        pl.BlockSpec((tm, D_MODEL), lambda t: (jnp.minimum(t, n_tiles - 1), 0)),
        pl.BlockSpec((tm, D_MODEL), lambda t: (jnp.maximum(t - 1, 0), 0)),
        lay((6, D_MODEL)),
    ] + [w_specs[name] for name in _MIX_WEIGHTS]
    act_scratch = [pltpu.VMEM((tm, width), dt) for _, width, dt in _ACT_SHAPES]
    return pl.pallas_call(
        functools.partial(_mixer_kernel, tiles_per_row=tiles_per_row),
        grid=(n_tiles + 1,),
        in_specs=in_specs,
        out_specs=pl.BlockSpec((tm, D_MODEL), lambda t: (jnp.maximum(t - 1, 0), 0)),
        out_shape=jax.ShapeDtypeStruct(x2d.shape, F32),
        scratch_shapes=act_scratch + act_scratch + [
            pltpu.VMEM((SUBLANES, 2 * M_QK), F32),
            pltpu.VMEM((M_HEADS, M_DQK, M_DV), F32),
            pltpu.VMEM((M_HEADS, 1, M_DQK), F32),
            pltpu.VMEM((M_HEADS, 1, GATE_LANES), F32),
            pltpu.VMEM((tm, M_V), BF16),
            pltpu.VMEM((tm, G_WIDTH), BF16),
            pltpu.VMEM((tm, M_QK), F32),
            pltpu.VMEM((tm, M_QK), F32),
            pltpu.VMEM((tm, GATE_LANES), F32),
            pltpu.VMEM((tm, M_V), F32),
            pltpu.VMEM((tm, G_WIDTH), F32),
            pltpu.VMEM((tm, G_WIDTH), BF16),
        ],
        compiler_params=pltpu.CompilerParams(
            dimension_semantics=("arbitrary",), vmem_limit_bytes=VMEM_LIMIT_BYTES),
        name="mixer",
    )(x2d, x2d, gains, *[w[name] for name in _MIX_WEIGHTS])


PACK_ROWS = 256


def _pack_w_in_kernel(w_ref, o_ref):
    gate_lo = 2 * M_QK + M_V
    n_gate = 2 * M_HEADS
    tail_lo = (w_ref.shape[-1] // LANES) * LANES
    o_ref[:, :gate_lo] = w_ref[:, :gate_lo].astype(BF16)
    lane = lax.broadcasted_iota(jnp.int32, (PACK_ROWS, GATE_LANES), 1)
    o_ref[:, gate_lo:gate_lo + GATE_LANES] = jnp.where(
        lane < n_gate, w_ref[:, gate_lo:gate_lo + GATE_LANES], 0.0).astype(BF16)
    rest = jnp.concatenate([w_ref[:, gate_lo:tail_lo][:, n_gate:], w_ref[:, tail_lo:]], axis=1)
    o_ref[:, gate_lo + GATE_LANES:] = rest.astype(BF16)


def _pack_w_in(w_in):
    depth, rows, cols = w_in.shape
    return pl.pallas_call(
        _pack_w_in_kernel,
        grid=(depth, rows // PACK_ROWS),
        in_specs=[pl.BlockSpec((None, PACK_ROWS, cols), lambda l, i: (l, i, 0))],
        out_specs=pl.BlockSpec((None, PACK_ROWS, _H_WIDTH), lambda l, i: (l, i, 0)),
        out_shape=jax.ShapeDtypeStruct((depth, rows, _H_WIDTH), BF16),
        compiler_params=pltpu.CompilerParams(
            dimension_semantics=("arbitrary", "arbitrary"), vmem_limit_bytes=VMEM_LIMIT_BYTES),
        name="pack_w_in",
    )(w_in)


def _pad_cols(w):
    return jnp.pad(w, ((0, 0), (0, 0), (0, PAD_COLS))).astype(BF16)


def kernel(x, norm_gains, ffn1_w_in, ffn1_w_out, w_in, conv_w, conv_b, i_bias, f_bias, m_norm_g, g_norm_g, g_norm_b, w_s, b_s, p_a, p_b, w_out, ffn2_w_in, ffn2_w_out):
    b, s, d = x.shape
    w = dict(
        h=_pack_w_in(w_in),
        p=_pad_cols(jnp.concatenate([p_a, p_b, w_out], axis=-1)),
        conv_w=conv_w,
        conv_b=conv_b[:, None, :],
        if_bias=jnp.pad(jnp.concatenate([i_bias, f_bias], axis=-1),
                        ((0, 0), (0, GATE_LANES - 2 * M_HEADS)))[:, None, :],
        m_norm_g=m_norm_g[:, None, :],
        g_norm_g=g_norm_g[:, None, :],
        g_norm_b=g_norm_b[:, None, :],
        w_s=w_s,
        b_s_t=jnp.swapaxes(b_s, 1, 2),
    )
    f1_in, f1_out = ffn1_w_in.astype(BF16), _pad_cols(ffn1_w_out)
    f2_in, f2_out = ffn2_w_in.astype(BF16), _pad_cols(ffn2_w_out)

    x2d = x.reshape(b * s, d)
    for layer in range(DEPTH):
        x2d = _ffn(x2d, norm_gains, f1_in, f1_out, layer, 0, 1)
        x2d = _mixer(x2d, norm_gains, w, layer, s // MIX_TM)
        x2d = _ffn(x2d, norm_gains, f2_in, f2_out, layer, 4, 5)
    return x2d.reshape(b, s, d)
```

```python
import functools

import jax
import jax.numpy as jnp
from jax import lax
from jax.experimental import pallas as pl
from jax.experimental.pallas import tpu as pltpu

D_MODEL = 1024
DEPTH = 4
M_HEADS = 4
M_DQK = 128
M_DV = 256
M_QK = M_HEADS * M_DQK
M_V = M_HEADS * M_DV
CHUNK = 128
CONV_W = 4
G_GROUPS = 4
G_WIDTH = 1024
G_DG = G_WIDTH // G_GROUPS
D_FF = 2816
EPS = 1e-6
GATE_LANES = 128
SUBLANES = 8
MXU_COLS = 256

FFN_TM = 1024
FFN_ROWS = 256
MIX_TM = 256
VMEM_LIMIT_BYTES = 56 * 1024 * 1024

BF16 = jnp.bfloat16
F32 = jnp.float32


def _dot(a, b):
    return jnp.dot(a, b, preferred_element_type=F32)


def _rms(x, g):
    return x * lax.rsqrt(jnp.mean(x * x, axis=-1, keepdims=True) + EPS) * g


def _ln(x, g):
    mu = jnp.mean(x, axis=-1, keepdims=True)
    xc = x - mu
    var = jnp.mean(xc * xc, axis=-1, keepdims=True)
    return xc * lax.rsqrt(var + EPS) * g


def _log_sigmoid(x):
    return jnp.minimum(x, 0.0) - jnp.log1p(jnp.exp(-jnp.abs(x)))


def _ffn_kernel(x_ref, gains_ref, w_in_ref, w_out_ref, o_ref, *, gi, go):
    for r in range(0, FFN_TM, FFN_ROWS):
        rows = slice(r, r + FFN_ROWS)
        x = x_ref[rows, :]
        h = _rms(x, gains_ref[gi:gi + 1, :]).astype(BF16)
        a = _dot(h, w_in_ref[:, :D_FF])
        g = _dot(h, w_in_ref[:, D_FF:])
        s = (jax.nn.silu(a) * g).astype(BF16)
        y = _dot(s, w_out_ref[:, :D_MODEL])
        o_ref[rows, :] = x + 0.5 * _rms(y, gains_ref[go:go + 1, :])


def _resident(shape, index_map):
    return pl.BlockSpec(shape, index_map, pipeline_mode=pl.Buffered(1))


def _ffn(x2d, gains, w_in, w_out, layer, gi, go):
    t = x2d.shape[0]
    return pl.pallas_call(
        functools.partial(_ffn_kernel, gi=gi, go=go),
        grid=(t // FFN_TM,),
        in_specs=[
            pl.BlockSpec((FFN_TM, D_MODEL), lambda i: (i, 0)),
            _resident((None, 6, D_MODEL), lambda i: (layer, 0, 0)),
            _resident((None, D_MODEL, 2 * D_FF), lambda i: (layer, 0, 0)),
            _resident((None, D_FF, FFN_OUT_WIDTH), lambda i: (layer, 0, 0)),
        ],
        out_specs=pl.BlockSpec((FFN_TM, D_MODEL), lambda i: (i, 0)),
        out_shape=jax.ShapeDtypeStruct(x2d.shape, F32),
        compiler_params=pltpu.CompilerParams(
            dimension_semantics=("arbitrary",), vmem_limit_bytes=VMEM_LIMIT_BYTES),
        name="ffn",
    )(x2d, gains, w_in, w_out)


_ACT_SHAPES = (
    ("qk", 2 * M_QK, F32), ("if", GATE_LANES, F32), ("v", M_V, BF16), ("o", M_V, F32),
    ("u", G_WIDTH, F32), ("vg", G_WIDTH, F32), ("ga", D_MODEL, F32), ("gb", D_MODEL, F32),
)
_N_ACT = len(_ACT_SHAPES)
_MIX_WEIGHTS = ("h", "p", "conv_w", "conv_b", "if_bias",
                "m_norm_g", "g_norm_g", "g_norm_b", "w_s", "b_s_t")

LANES = 128
PAD_COLS = LANES


def _col_ranges(names_widths):
    out, off = {}, 0
    for name, width in names_widths:
        out[name] = (off, off + width)
        off += width
    return out, off


_H_COLS, _H_WIDTH = _col_ranges((("qk", 2 * M_QK), ("v", M_V), ("if", GATE_LANES), ("o", M_V),
                                 ("u", G_WIDTH), ("vg", G_WIDTH), ("ga", D_MODEL), ("gb", D_MODEL)))
_P_COLS, _P_WIDTH = _col_ranges((("p_a", D_MODEL), ("p_b", D_MODEL), ("w_out", D_MODEL),
                                 ("pad", PAD_COLS)))
FFN_OUT_WIDTH = D_MODEL + PAD_COLS


def _chunk_cumsum(v, row_in_chunk):
    shift = 1
    while shift < CHUNK:
        rolled = pltpu.roll(v, shift, axis=0)
        v = v + jnp.where(row_in_chunk >= shift, rolled, 0.0)
        shift *= 2
    return v


def _shift_rows(cur, tail, s, row_in_vreg):
    rolled = pltpu.roll(cur, s, axis=0)
    head = jnp.where(row_in_vreg < s, pltpu.roll(tail, s, axis=0), rolled[0:SUBLANES])
    return jnp.concatenate([head, rolled[SUBLANES:]], axis=0)


def _mixer_step(xn_ref, xb_ref, gains_ref, w, o_ref, act_w, act_r, h_cur_ref, h_next_ref,
                tail_ref, ct_ref, n_ref, m_ref, ya_ref, hg_ref, q_ref, k_ref, gates_ref,
                og_ref, u_ref, vln_ref):
    tm = MIX_TM
    nch = tm // CHUNK

    tri_r = lax.broadcasted_iota(jnp.int32, (CHUNK, CHUNK), 0)
    tri_c = lax.broadcasted_iota(jnp.int32, (CHUNK, CHUNK), 1)
    causal = tri_r >= tri_c

    def mlstm_unit(c, hd):
        rows = slice(c * CHUNK, (c + 1) * CHUNK)
        qk_cols = slice(hd * M_DQK, (hd + 1) * M_DQK)
        v_cols = slice(hd * M_DV, (hd + 1) * M_DV)
        gc = gates_ref[rows, :]
        gct = gc.T
        q = q_ref[rows, qk_cols]
        k = k_ref[rows, qk_cols]
        v = act_r["v"][rows, v_cols]
        i_col = gc[:, hd:hd + 1]
        b_col = gc[:, M_HEADS + hd:M_HEADS + hd + 1]
        i_row = gct[hd:hd + 1, :]
        b_row = gct[M_HEADS + hd:M_HEADS + hd + 1, :]
        b_last = b_row[:, CHUNK - 1:CHUNK]
        m_prev = m_ref[hd][:, 0:1]
        ct = ct_ref[hd]
        n_row = n_ref[hd]

        a_row = i_row - b_row
        d = jnp.where(causal, b_col + a_row, -jnp.inf)
        inter = b_col + m_prev
        m_t = jnp.maximum(inter, jnp.max(d, axis=-1, keepdims=True))
        s = lax.dot_general(q.astype(BF16), k.astype(BF16), (((1,), (1,)), ((), ())),
                            preferred_element_type=F32)
        p = jnp.exp(d - m_t) * s
        s_inter = jnp.exp(inter - m_t)
        lhs = jnp.concatenate([p, s_inter * q], axis=1).astype(BF16)
        rhs = jnp.concatenate([v, ct.astype(BF16)], axis=0)
        num = _dot(lhs, rhs)
        den = (jnp.sum(p, axis=-1, keepdims=True)
               + s_inter * jnp.sum(q * n_row, axis=-1, keepdims=True))
        hh = num / jnp.maximum(jnp.abs(den), jnp.exp(-m_t))

        g_row = b_last + a_row
        m_new = jnp.maximum(b_last + m_prev, jnp.max(g_row, axis=-1, keepdims=True))
        w_col = jnp.exp(b_last - b_col + i_col - m_new)
        decay = jnp.exp(b_last + m_prev - m_new)
        kw = w_col * k
        ct_ref[hd] = decay * ct + lax.dot_general(
            kw.astype(BF16), v, (((0,), (0,)), ((), ())), preferred_element_type=F32)
        n_ref[hd] = decay * n_row + jnp.sum(kw, axis=0, keepdims=True)
        m_ref[hd] = jnp.broadcast_to(m_new, (1, GATE_LANES))

        hm = _ln(hh, w["m_norm_g"][:, v_cols])
        ya_ref[rows, v_cols] = (og_ref[rows, v_cols] * hm).astype(BF16)

    def gmlp_group(grp):
        w_tri = jnp.where(causal, w["w_s"][grp], 0.0).astype(BF16)
        bias_col = w["b_s_t"][:, grp:grp + 1]
        cols = slice(grp * G_DG, (grp + 1) * G_DG)
        u_ref[:, cols] = jax.nn.gelu(act_r["u"][:, cols])
        for c in range(nch):
            rows = slice(c * CHUNK, (c + 1) * CHUNK)
            sg = _dot(w_tri, vln_ref[rows, cols]) + bias_col
            hg_ref[rows, cols] = (u_ref[rows, cols] * sg).astype(BF16)

    def wp(name):
        lo, hi = _P_COLS[name]
        return w["p"][:, lo:hi]

    def proj_block(name, lo, hi):
        base = _H_COLS[name][0]
        z = _dot(h_cur_ref[...], w["h"][:, base + lo:base + hi])
        act_w[name][:, lo:hi] = z.astype(act_w[name].dtype)

    def gates():
        pre = act_r["if"][...] + w["if_bias"][...]
        col = lax.broadcasted_iota(jnp.int32, (tm, GATE_LANES), 1)
        row_in_chunk = lax.broadcasted_iota(jnp.int32, (tm, GATE_LANES), 0) % CHUNK
        gates_ref[...] = jnp.where(
            col >= M_HEADS, _chunk_cumsum(_log_sigmoid(pre), row_in_chunk), pre)

    def conv_block(lo, hi):
        zqk = act_r["qk"][:, lo:hi]
        row_in_vreg = lax.broadcasted_iota(jnp.int32, (SUBLANES, hi - lo), 0)
        tail = tail_ref[:, lo:hi]
        conv = w["conv_b"][:, lo:hi] + w["conv_w"][CONV_W - 1:CONV_W, lo:hi] * zqk
        for s in range(1, CONV_W):
            j = CONV_W - 1 - s
            conv = conv + w["conv_w"][j:j + 1, lo:hi] * _shift_rows(zqk, tail, s, row_in_vreg)
        tail_ref[:, lo:hi] = zqk[tm - SUBLANES:tm, :]
        y = jax.nn.silu(conv)
        if lo < M_QK:
            q_ref[:, lo:hi] = y * (M_DQK ** -0.5)
        else:
            k_ref[:, lo - M_QK:hi - M_QK] = y

    def out_gate(hd):
        v_cols = slice(hd * M_DV, (hd + 1) * M_DV)
        og_ref[:, v_cols] = jax.nn.sigmoid(act_r["o"][:, v_cols])

    def gmlp_norm():
        vln_ref[...] = (_ln(jax.nn.gelu(act_r["vg"][...]), w["g_norm_g"][...])
                        + w["g_norm_b"][...]).astype(BF16)

    proj = []
    for name, width, _ in _ACT_SHAPES:
        for lo in range(0, width, MXU_COLS):
            hi = min(lo + MXU_COLS, width)
            proj.append((hi - lo, functools.partial(proj_block, name, lo, hi)))
    rec = [(300, gates)]
    rec += [(350, functools.partial(conv_block, lo, lo + MXU_COLS))
            for lo in range(0, 2 * M_QK, MXU_COLS)]
    rec += [(100, functools.partial(out_gate, hd)) for hd in range(M_HEADS)]
    rec += [(450, functools.partial(mlstm_unit, 0, hd)) for hd in range(M_HEADS)]
    rec += [(900, gmlp_norm)]
    rec += [(300, functools.partial(gmlp_group, grp)) for grp in range(G_GROUPS)]
    rec += [(450, functools.partial(mlstm_unit, c, hd))
            for c in range(1, nch) for hd in range(M_HEADS)]
    proj_total = sum(c for c, _ in proj)
    rec_total = sum(c for c, _ in rec)
    done_proj, done_rec, ip = 0, 0, 0
    for cost, run in rec:
        while ip < len(proj) and done_proj * rec_total <= done_rec * proj_total:
            done_proj += proj[ip][0]
            proj[ip][1]()
            ip += 1
        run()
        done_rec += cost
    for _, run in proj[ip:]:
        run()

    h_next_ref[...] = _rms(xn_ref[...], gains_ref[2:3, :]).astype(BF16)
    y_b = _dot(hg_ref[...], wp("p_b"))
    y_a = _dot(ya_ref[...], wp("p_a"))
    merged = jax.nn.sigmoid(act_r["ga"][...]) * y_a + jax.nn.sigmoid(act_r["gb"][...]) * y_b
    mix = _dot(merged.astype(BF16), wp("w_out"))
    o_ref[...] = xb_ref[...] + _rms(mix, gains_ref[3:4, :])


def _mixer_kernel(*refs, tiles_per_row):
    xn_ref, xb_ref, gains_ref = refs[:3]
    nw = len(_MIX_WEIGHTS)
    w = dict(zip(_MIX_WEIGHTS, refs[3:3 + nw]))
    o_ref = refs[3 + nw]
    scratch = refs[4 + nw:]
    act = [dict(zip((n for n, _, _ in _ACT_SHAPES), scratch[i * _N_ACT:(i + 1) * _N_ACT]))
           for i in range(2)]
    h_refs = scratch[2 * _N_ACT:2 * _N_ACT + 2]
    rest = scratch[2 * _N_ACT + 2:]
    tail_ref, ct_ref, n_ref, m_ref = rest[:4]
    t = pl.program_id(0)

    @pl.when(t == 0)
    def _():
        for ref in act[1].values():
            ref[...] = jnp.zeros_like(ref)
        h_refs[0][...] = _rms(xb_ref[...], gains_ref[2:3, :]).astype(BF16)

    @pl.when(jnp.logical_or(t == 0, (t + tiles_per_row - 1) % tiles_per_row == 0))
    def _():
        tail_ref[...] = jnp.zeros_like(tail_ref)
        ct_ref[...] = jnp.zeros_like(ct_ref)
        n_ref[...] = jnp.zeros_like(n_ref)
        m_ref[...] = jnp.zeros_like(m_ref)

    step = functools.partial(_mixer_step, xn_ref, xb_ref, gains_ref, w, o_ref)

    @pl.when(t % 2 == 0)
    def _():
        step(act[0], act[1], h_refs[0], h_refs[1], *rest)

    @pl.when(t % 2 == 1)
    def _():
        step(act[1], act[0], h_refs[1], h_refs[0], *rest)


def _mixer(x2d, gains, w, layer, tiles_per_row):
    tm = MIX_TM
    n_tiles = x2d.shape[0] // tm

    def lay(shape):
        nd = len(shape)
        return _resident((None,) + shape, lambda t: (layer,) + (0,) * nd)

    w_specs = {
        "h": lay((D_MODEL, _H_WIDTH)), "p": lay((D_MODEL, _P_WIDTH)),
        "conv_w": lay((CONV_W, 2 * M_QK)), "conv_b": lay((1, 2 * M_QK)),
        "if_bias": lay((1, GATE_LANES)), "m_norm_g": lay((1, M_V)),
        "g_norm_g": lay((1, G_WIDTH)), "g_norm_b": lay((1, G_WIDTH)),
        "w_s": lay((G_GROUPS, CHUNK, CHUNK)), "b_s_t": lay((CHUNK, G_GROUPS)),
    }
    in_specs = [
        pl.BlockSpec((tm, D_MODEL), lambda t: (jnp.minimum(t + 1, n_tiles - 1), 0)),
        pl.BlockSpec((tm, D_MODEL), lambda t: (jnp.maximum(t - 1, 0), 0)),
        lay((6, D_MODEL)),
    ] + [w_specs[name] for name in _MIX_WEIGHTS]
    act_scratch = [pltpu.VMEM((tm, width), dt) for _, width, dt in _ACT_SHAPES]
    return pl.pallas_call(
        functools.partial(_mixer_kernel, tiles_per_row=tiles_per_row),
        grid=(n_tiles + 1,),
        in_specs=in_specs,
        out_specs=pl.BlockSpec((tm, D_MODEL), lambda t: (jnp.maximum(t - 1, 0), 0)),
        out_shape=jax.ShapeDtypeStruct(x2d.shape, F32),
        scratch_shapes=act_scratch + act_scratch + [
            pltpu.VMEM((tm, D_MODEL), BF16),
            pltpu.VMEM((tm, D_MODEL), BF16),
            pltpu.VMEM((SUBLANES, 2 * M_QK), F32),
            pltpu.VMEM((M_HEADS, M_DQK, M_DV), F32),
            pltpu.VMEM((M_HEADS, 1, M_DQK), F32),
            pltpu.VMEM((M_HEADS, 1, GATE_LANES), F32),
            pltpu.VMEM((tm, M_V), BF16),
            pltpu.VMEM((tm, G_WIDTH), BF16),
            pltpu.VMEM((tm, M_QK), F32),
            pltpu.VMEM((tm, M_QK), F32),
            pltpu.VMEM((tm, GATE_LANES), F32),
            pltpu.VMEM((tm, M_V), F32),
            pltpu.VMEM((tm, G_WIDTH), F32),
            pltpu.VMEM((tm, G_WIDTH), BF16),
        ],
        compiler_params=pltpu.CompilerParams(
            dimension_semantics=("arbitrary",), vmem_limit_bytes=VMEM_LIMIT_BYTES),
        name="mixer",
    )(x2d, x2d, gains, *[w[name] for name in _MIX_WEIGHTS])


PACK_ROWS = 256


def _pack_w_in_kernel(w_ref, o_ref):
    gate_lo = 2 * M_QK + M_V
    n_gate = 2 * M_HEADS
    tail_lo = (w_ref.shape[-1] // LANES) * LANES
    o_ref[:, :gate_lo] = w_ref[:, :gate_lo]
    lane = lax.broadcasted_iota(jnp.int32, (PACK_ROWS, GATE_LANES), 1)
    o_ref[:, gate_lo:gate_lo + GATE_LANES] = jnp.where(
        lane < n_gate, w_ref[:, gate_lo:gate_lo + GATE_LANES], jnp.zeros((), BF16))
    o_ref[:, gate_lo + GATE_LANES:] = jnp.concatenate(
        [w_ref[:, gate_lo:tail_lo][:, n_gate:], w_ref[:, tail_lo:]], axis=1)


def _pack_w_in(w_in):
    depth, rows, cols = w_in.shape
    return pl.pallas_call(
        _pack_w_in_kernel,
        grid=(depth, rows // PACK_ROWS),
        in_specs=[pl.BlockSpec((None, PACK_ROWS, cols), lambda l, i: (l, i, 0))],
        out_specs=pl.BlockSpec((None, PACK_ROWS, _H_WIDTH), lambda l, i: (l, i, 0)),
        out_shape=jax.ShapeDtypeStruct((depth, rows, _H_WIDTH), BF16),
        compiler_params=pltpu.CompilerParams(
            dimension_semantics=("arbitrary", "arbitrary"), vmem_limit_bytes=VMEM_LIMIT_BYTES),
        name="pack_w_in",
    )(w_in.astype(BF16))


def _pad_cols(w):
    return jnp.pad(w, ((0, 0), (0, 0), (0, PAD_COLS))).astype(BF16)


def kernel(x, norm_gains, ffn1_w_in, ffn1_w_out, w_in, conv_w, conv_b, i_bias, f_bias, m_norm_g, g_norm_g, g_norm_b, w_s, b_s, p_a, p_b, w_out, ffn2_w_in, ffn2_w_out):
    b, s, d = x.shape
    w = dict(
        h=_pack_w_in(w_in),
        p=_pad_cols(jnp.concatenate([p_a, p_b, w_out], axis=-1)),
        conv_w=conv_w,
        conv_b=conv_b[:, None, :],
        if_bias=jnp.pad(jnp.concatenate([i_bias, f_bias], axis=-1),
                        ((0, 0), (0, GATE_LANES - 2 * M_HEADS)))[:, None, :],
        m_norm_g=m_norm_g[:, None, :],
        g_norm_g=g_norm_g[:, None, :],
        g_norm_b=g_norm_b[:, None, :],
        w_s=w_s,
        b_s_t=jnp.swapaxes(b_s, 1, 2),
    )
    f1_in, f1_out = ffn1_w_in.astype(BF16), _pad_cols(ffn1_w_out)
    f2_in, f2_out = ffn2_w_in.astype(BF16), _pad_cols(ffn2_w_out)

    x2d = x.reshape(b * s, d)
    for layer in range(DEPTH):
        x2d = _ffn(x2d, norm_gains, f1_in, f1_out, layer, 0, 1)
        x2d = _mixer(x2d, norm_gains, w, layer, s // MIX_TM)
        x2d = _ffn(x2d, norm_gains, f2_in, f2_out, layer, 4, 5)
    return x2d.reshape(b, s, d)
```

```python
import functools

import jax
import jax.numpy as jnp
from jax import lax
from jax.experimental import pallas as pl
from jax.experimental.pallas import tpu as pltpu

D_MODEL = 1024
DEPTH = 4
M_HEADS = 4
M_DQK = 128
M_DV = 256
M_QK = M_HEADS * M_DQK
M_V = M_HEADS * M_DV
CHUNK = 128
CONV_W = 4
G_GROUPS = 4
G_WIDTH = 1024
G_DG = G_WIDTH // G_GROUPS
D_FF = 2816
EPS = 1e-6
GATE_LANES = 128
SUBLANES = 8
MXU_COLS = 256

FFN_TM = 512
FFN_ROWS = 256
FFN_STAGE_ROWS = 128
MIX_TM = 256
VMEM_LIMIT_BYTES = 56 * 1024 * 1024

BF16 = jnp.bfloat16
F32 = jnp.float32


def _dot(a, b):
    return jnp.dot(a, b, preferred_element_type=F32)


def _rms(x, g):
    return x * lax.rsqrt(jnp.mean(x * x, axis=-1, keepdims=True) + EPS) * g


def _ln(x, g):
    mu = jnp.mean(x, axis=-1, keepdims=True)
    xc = x - mu
    var = jnp.mean(xc * xc, axis=-1, keepdims=True)
    return xc * lax.rsqrt(var + EPS) * g


def _log_sigmoid(x):
    return jnp.minimum(x, 0.0) - jnp.log1p(jnp.exp(-jnp.abs(x)))


def _load_cast(src_hbm, dst_ref, stage_ref, sems):
    chunk = stage_ref.shape[1]
    rows, cols = src_hbm.shape
    n = rows // chunk

    def copy(k):
        return pltpu.make_async_copy(
            src_hbm.at[pl.ds(k * chunk, chunk), :], stage_ref.at[k % 2], sems.at[k % 2])

    copy(0).start()
    for k in range(n):
        if k + 1 < n:
            copy(k + 1).start()
        copy(k).wait()
        dst_ref[k * chunk:(k + 1) * chunk, :cols] = stage_ref[k % 2].astype(BF16)


def _ffn_kernel(x_ref, gains_ref, w_in_hbm, w_out_hbm, o_ref, w_in_ref, w_out_ref,
                stage_in, stage_out, sems_in, sems_out, *, layer, gi, go):
    @pl.when(pl.program_id(0) == 0)
    def _():
        _load_cast(w_in_hbm.at[layer], w_in_ref, stage_in, sems_in)
        _load_cast(w_out_hbm.at[layer], w_out_ref, stage_out, sems_out)

    for r in range(0, FFN_TM, FFN_ROWS):
        rows = slice(r, r + FFN_ROWS)
        x = x_ref[rows, :]
        h = _rms(x, gains_ref[gi:gi + 1, :]).astype(BF16)
        a = _dot(h, w_in_ref[:, :D_FF])
        g = _dot(h, w_in_ref[:, D_FF:])
        s = (jax.nn.silu(a) * g).astype(BF16)
        y = _dot(s, w_out_ref[:, :D_MODEL])
        o_ref[rows, :] = x + 0.5 * _rms(y, gains_ref[go:go + 1, :])


def _resident(shape, index_map):
    return pl.BlockSpec(shape, index_map, pipeline_mode=pl.Buffered(1))


def _ffn(x2d, gains, w_in, w_out, layer, gi, go):
    t = x2d.shape[0]
    return pl.pallas_call(
        functools.partial(_ffn_kernel, layer=layer, gi=gi, go=go),
        grid=(t // FFN_TM,),
        in_specs=[
            pl.BlockSpec((FFN_TM, D_MODEL), lambda i: (i, 0)),
            _resident((None, 6, D_MODEL), lambda i: (layer, 0, 0)),
            pl.BlockSpec(memory_space=pl.ANY),
            pl.BlockSpec(memory_space=pl.ANY),
        ],
        out_specs=pl.BlockSpec((FFN_TM, D_MODEL), lambda i: (i, 0)),
        out_shape=jax.ShapeDtypeStruct(x2d.shape, F32),
        scratch_shapes=[
            pltpu.VMEM((D_MODEL, 2 * D_FF), BF16),
            pltpu.VMEM((D_FF, FFN_OUT_WIDTH), BF16),
            pltpu.VMEM((2, FFN_STAGE_ROWS, 2 * D_FF), F32),
            pltpu.VMEM((2, FFN_STAGE_ROWS, D_MODEL), F32),
            pltpu.SemaphoreType.DMA((2,)),
            pltpu.SemaphoreType.DMA((2,)),
        ],
        compiler_params=pltpu.CompilerParams(
            dimension_semantics=("arbitrary",), vmem_limit_bytes=VMEM_LIMIT_BYTES),
        name="ffn",
    )(x2d, gains, w_in, w_out)


_ACT_SHAPES = (
    ("qk", 2 * M_QK, F32), ("if", GATE_LANES, F32), ("v", M_V, BF16), ("o", M_V, F32),
    ("u", G_WIDTH, F32), ("vg", G_WIDTH, F32), ("ga", D_MODEL, F32), ("gb", D_MODEL, F32),
)
_N_ACT = len(_ACT_SHAPES)
_MIX_WEIGHTS = ("h", "p", "conv_w", "conv_b", "if_bias",
                "m_norm_g", "g_norm_g", "g_norm_b", "w_s", "b_s_t")

LANES = 128
PAD_COLS = LANES


def _col_ranges(names_widths):
    out, off = {}, 0
    for name, width in names_widths:
        out[name] = (off, off + width)
        off += width
    return out, off


_H_COLS, _H_WIDTH = _col_ranges((("qk", 2 * M_QK), ("v", M_V), ("if", GATE_LANES), ("o", M_V),
                                 ("u", G_WIDTH), ("vg", G_WIDTH), ("ga", D_MODEL), ("gb", D_MODEL)))
_P_COLS, _P_WIDTH = _col_ranges((("p_a", D_MODEL), ("p_b", D_MODEL), ("w_out", D_MODEL),
                                 ("pad", PAD_COLS)))
FFN_OUT_WIDTH = D_MODEL + PAD_COLS


def _chunk_cumsum(v, row_in_chunk):
    shift = 1
    while shift < CHUNK:
        rolled = pltpu.roll(v, shift, axis=0)
        v = v + jnp.where(row_in_chunk >= shift, rolled, 0.0)
        shift *= 2
    return v


def _shift_rows(cur, tail, s, row_in_vreg):
    rolled = pltpu.roll(cur, s, axis=0)
    head = jnp.where(row_in_vreg < s, pltpu.roll(tail, s, axis=0), rolled[0:SUBLANES])
    return jnp.concatenate([head, rolled[SUBLANES:]], axis=0)


def _mixer_step(xn_ref, xb_ref, gains_ref, w, o_ref, act_w, act_r, h_cur_ref, h_next_ref,
                tail_ref, ct_ref, n_ref, m_ref, ya_ref, hg_ref, q_ref, k_ref, gates_ref,
                og_ref, u_ref, vln_ref):
    tm = MIX_TM
    nch = tm // CHUNK

    tri_r = lax.broadcasted_iota(jnp.int32, (CHUNK, CHUNK), 0)
    tri_c = lax.broadcasted_iota(jnp.int32, (CHUNK, CHUNK), 1)
    causal = tri_r >= tri_c

    def mlstm_unit(c, hd):
        rows = slice(c * CHUNK, (c + 1) * CHUNK)
        qk_cols = slice(hd * M_DQK, (hd + 1) * M_DQK)
        v_cols = slice(hd * M_DV, (hd + 1) * M_DV)
        gc = gates_ref[rows, :]
        gct = gc.T
        q = q_ref[rows, qk_cols]
        k = k_ref[rows, qk_cols]
        v = act_r["v"][rows, v_cols]
        i_col = gc[:, hd:hd + 1]
        b_col = gc[:, M_HEADS + hd:M_HEADS + hd + 1]
        i_row = gct[hd:hd + 1, :]
        b_row = gct[M_HEADS + hd:M_HEADS + hd + 1, :]
        b_last = b_row[:, CHUNK - 1:CHUNK]
        m_prev = m_ref[hd][:, 0:1]
        ct = ct_ref[hd]
        n_row = n_ref[hd]

        a_row = i_row - b_row
        d = jnp.where(causal, b_col + a_row, -jnp.inf)
        inter = b_col + m_prev
        m_t = jnp.maximum(inter, jnp.max(d, axis=-1, keepdims=True))
        s = lax.dot_general(q.astype(BF16), k.astype(BF16), (((1,), (1,)), ((), ())),
                            preferred_element_type=F32)
        p = jnp.exp(d - m_t) * s
        s_inter = jnp.exp(inter - m_t)
        lhs = jnp.concatenate([p, s_inter * q], axis=1).astype(BF16)
        rhs = jnp.concatenate([v, ct.astype(BF16)], axis=0)
        num = _dot(lhs, rhs)
        den = (jnp.sum(p, axis=-1, keepdims=True)
               + s_inter * jnp.sum(q * n_row, axis=-1, keepdims=True))
        hh = num / jnp.maximum(jnp.abs(den), jnp.exp(-m_t))

        g_row = b_last + a_row
        m_new = jnp.maximum(b_last + m_prev, jnp.max(g_row, axis=-1, keepdims=True))
        w_col = jnp.exp(b_last - b_col + i_col - m_new)
        decay = jnp.exp(b_last + m_prev - m_new)
        kw = w_col * k
        ct_ref[hd] = decay * ct + lax.dot_general(
            kw.astype(BF16), v, (((0,), (0,)), ((), ())), preferred_element_type=F32)
        n_ref[hd] = decay * n_row + jnp.sum(kw, axis=0, keepdims=True)
        m_ref[hd] = jnp.broadcast_to(m_new, (1, GATE_LANES))

        hm = _ln(hh, w["m_norm_g"][:, v_cols])
        ya_ref[rows, v_cols] = (og_ref[rows, v_cols] * hm).astype(BF16)

    def gmlp_group(grp):
        w_tri = jnp.where(causal, w["w_s"][grp], 0.0).astype(BF16)
        bias_col = w["b_s_t"][:, grp:grp + 1]
        cols = slice(grp * G_DG, (grp + 1) * G_DG)
        u_ref[:, cols] = jax.nn.gelu(act_r["u"][:, cols])
        for c in range(nch):
            rows = slice(c * CHUNK, (c + 1) * CHUNK)
            sg = _dot(w_tri, vln_ref[rows, cols]) + bias_col
            hg_ref[rows, cols] = (u_ref[rows, cols] * sg).astype(BF16)

    def wp(name):
        lo, hi = _P_COLS[name]
        return w["p"][:, lo:hi]

    def proj_block(name, lo, hi):
        base = _H_COLS[name][0]
        z = _dot(h_cur_ref[...], w["h"][:, base + lo:base + hi])
        act_w[name][:, lo:hi] = z.astype(act_w[name].dtype)

    def gates():
        pre = act_r["if"][...] + w["if_bias"][...]
        col = lax.broadcasted_iota(jnp.int32, (tm, GATE_LANES), 1)
        row_in_chunk = lax.broadcasted_iota(jnp.int32, (tm, GATE_LANES), 0) % CHUNK
        gates_ref[...] = jnp.where(
            col >= M_HEADS, _chunk_cumsum(_log_sigmoid(pre), row_in_chunk), pre)

    def conv_block(lo, hi):
        zqk = act_r["qk"][:, lo:hi]
        row_in_vreg = lax.broadcasted_iota(jnp.int32, (SUBLANES, hi - lo), 0)
        tail = tail_ref[:, lo:hi]
        conv = w["conv_b"][:, lo:hi] + w["conv_w"][CONV_W - 1:CONV_W, lo:hi] * zqk
        for s in range(1, CONV_W):
            j = CONV_W - 1 - s
            conv = conv + w["conv_w"][j:j + 1, lo:hi] * _shift_rows(zqk, tail, s, row_in_vreg)
        tail_ref[:, lo:hi] = zqk[tm - SUBLANES:tm, :]
        y = jax.nn.silu(conv)
        if lo < M_QK:
            q_ref[:, lo:hi] = y * (M_DQK ** -0.5)
        else:
            k_ref[:, lo - M_QK:hi - M_QK] = y

    def out_gate(hd):
        v_cols = slice(hd * M_DV, (hd + 1) * M_DV)
        og_ref[:, v_cols] = jax.nn.sigmoid(act_r["o"][:, v_cols])

    def gmlp_norm():
        vln_ref[...] = (_ln(jax.nn.gelu(act_r["vg"][...]), w["g_norm_g"][...])
                        + w["g_norm_b"][...]).astype(BF16)

    proj = []
    for name, width, _ in _ACT_SHAPES:
        for lo in range(0, width, MXU_COLS):
            hi = min(lo + MXU_COLS, width)
            proj.append((hi - lo, functools.partial(proj_block, name, lo, hi)))
    rec = [(300, gates)]
    rec += [(350, functools.partial(conv_block, lo, lo + MXU_COLS))
            for lo in range(0, 2 * M_QK, MXU_COLS)]
    rec += [(100, functools.partial(out_gate, hd)) for hd in range(M_HEADS)]
    rec += [(450, functools.partial(mlstm_unit, 0, hd)) for hd in range(M_HEADS)]
    rec += [(900, gmlp_norm)]
    rec += [(300, functools.partial(gmlp_group, grp)) for grp in range(G_GROUPS)]
    rec += [(450, functools.partial(mlstm_unit, c, hd))
            for c in range(1, nch) for hd in range(M_HEADS)]
    proj_total = sum(c for c, _ in proj)
    rec_total = sum(c for c, _ in rec)
    done_proj, done_rec, ip = 0, 0, 0
    for cost, run in rec:
        while ip < len(proj) and done_proj * rec_total <= done_rec * proj_total:
            done_proj += proj[ip][0]
            proj[ip][1]()
            ip += 1
        run()
        done_rec += cost
    for _, run in proj[ip:]:
        run()

    h_next_ref[...] = _rms(xn_ref[...], gains_ref[2:3, :]).astype(BF16)
    y_b = _dot(hg_ref[...], wp("p_b"))
    y_a = _dot(ya_ref[...], wp("p_a"))
    merged = jax.nn.sigmoid(act_r["ga"][...]) * y_a + jax.nn.sigmoid(act_r["gb"][...]) * y_b
    mix = _dot(merged.astype(BF16), wp("w_out"))
    o_ref[...] = xb_ref[...] + _rms(mix, gains_ref[3:4, :])


def _mixer_kernel(*refs, tiles_per_row):
    xn_ref, xb_ref, gains_ref = refs[:3]
    nw = len(_MIX_WEIGHTS)
    w = dict(zip(_MIX_WEIGHTS, refs[3:3 + nw]))
    o_ref = refs[3 + nw]
    scratch = refs[4 + nw:]
    act = [dict(zip((n for n, _, _ in _ACT_SHAPES), scratch[i * _N_ACT:(i + 1) * _N_ACT]))
           for i in range(2)]
    h_refs = scratch[2 * _N_ACT:2 * _N_ACT + 2]
    rest = scratch[2 * _N_ACT + 2:]
    tail_ref, ct_ref, n_ref, m_ref = rest[:4]
    t = pl.program_id(0)

    @pl.when(t == 0)
    def _():
        for ref in act[1].values():
            ref[...] = jnp.zeros_like(ref)
        h_refs[0][...] = _rms(xb_ref[...], gains_ref[2:3, :]).astype(BF16)

    @pl.when(jnp.logical_or(t == 0, (t + tiles_per_row - 1) % tiles_per_row == 0))
    def _():
        tail_ref[...] = jnp.zeros_like(tail_ref)
        ct_ref[...] = jnp.zeros_like(ct_ref)
        n_ref[...] = jnp.zeros_like(n_ref)
        m_ref[...] = jnp.zeros_like(m_ref)

    step = functools.partial(_mixer_step, xn_ref, xb_ref, gains_ref, w, o_ref)

    @pl.when(t % 2 == 0)
    def _():
        step(act[0], act[1], h_refs[0], h_refs[1], *rest)

    @pl.when(t % 2 == 1)
    def _():
        step(act[1], act[0], h_refs[1], h_refs[0], *rest)


def _mixer(x2d, gains, w, layer, tiles_per_row):
    tm = MIX_TM
    n_tiles = x2d.shape[0] // tm

    def lay(shape):
        nd = len(shape)
        return _resident((None,) + shape, lambda t: (layer,) + (0,) * nd)

    w_specs = {
        "h": lay((D_MODEL, _H_WIDTH)), "p": lay((D_MODEL, _P_WIDTH)),
        "conv_w": lay((CONV_W, 2 * M_QK)), "conv_b": lay((1, 2 * M_QK)),
        "if_bias": lay((1, GATE_LANES)), "m_norm_g": lay((1, M_V)),
        "g_norm_g": lay((1, G_WIDTH)), "g_norm_b": lay((1, G_WIDTH)),
        "w_s": lay((G_GROUPS, CHUNK, CHUNK)), "b_s_t": lay((CHUNK, G_GROUPS)),
    }
    in_specs = [
        pl.BlockSpec((tm, D_MODEL), lambda t: (jnp.minimum(t + 1, n_tiles - 1), 0)),
        pl.BlockSpec((tm, D_MODEL), lambda t: (jnp.maximum(t - 1, 0), 0)),
        lay((6, D_MODEL)),
    ] + [w_specs[name] for name in _MIX_WEIGHTS]
    act_scratch = [pltpu.VMEM((tm, width), dt) for _, width, dt in _ACT_SHAPES]
    return pl.pallas_call(
        functools.partial(_mixer_kernel, tiles_per_row=tiles_per_row),
        grid=(n_tiles + 1,),
        in_specs=in_specs,
        out_specs=pl.BlockSpec((tm, D_MODEL), lambda t: (jnp.maximum(t - 1, 0), 0)),
        out_shape=jax.ShapeDtypeStruct(x2d.shape, F32),
        scratch_shapes=act_scratch + act_scratch + [
            pltpu.VMEM((tm, D_MODEL), BF16),
            pltpu.VMEM((tm, D_MODEL), BF16),
            pltpu.VMEM((SUBLANES, 2 * M_QK), F32),
            pltpu.VMEM((M_HEADS, M_DQK, M_DV), F32),
            pltpu.VMEM((M_HEADS, 1, M_DQK), F32),
            pltpu.VMEM((M_HEADS, 1, GATE_LANES), F32),
            pltpu.VMEM((tm, M_V), BF16),
            pltpu.VMEM((tm, G_WIDTH), BF16),
            pltpu.VMEM((tm, M_QK), F32),
            pltpu.VMEM((tm, M_QK), F32),
            pltpu.VMEM((tm, GATE_LANES), F32),
            pltpu.VMEM((tm, M_V), F32),
            pltpu.VMEM((tm, G_WIDTH), F32),
            pltpu.VMEM((tm, G_WIDTH), BF16),
        ],
        compiler_params=pltpu.CompilerParams(
            dimension_semantics=("arbitrary",), vmem_limit_bytes=VMEM_LIMIT_BYTES),
        name="mixer",
    )(x2d, x2d, gains, *[w[name] for name in _MIX_WEIGHTS])


PACK_ROWS = 256


def _pack_w_in_kernel(w_ref, o_ref):
    gate_lo = 2 * M_QK + M_V
    n_gate = 2 * M_HEADS
    o_ref[:, :gate_lo] = w_ref[:, :gate_lo]
    lane = lax.broadcasted_iota(jnp.int32, (PACK_ROWS, GATE_LANES), 1)
    o_ref[:, gate_lo:gate_lo + GATE_LANES] = jnp.where(
        lane < n_gate, w_ref[:, gate_lo:gate_lo + GATE_LANES], jnp.zeros((), BF16))
    rest = _H_WIDTH - gate_lo - GATE_LANES
    o_ref[:, gate_lo + GATE_LANES:] = w_ref[:, gate_lo:][:, n_gate:n_gate + rest]


def _pack_w_in(w_in):
    depth, rows, cols = w_in.shape
    padded = jnp.pad(w_in, ((0, 0), (0, 0), (0, _H_WIDTH - cols))).astype(BF16)
    return pl.pallas_call(
        _pack_w_in_kernel,
        grid=(depth, rows // PACK_ROWS),
        in_specs=[pl.BlockSpec((None, PACK_ROWS, _H_WIDTH), lambda l, i: (l, i, 0))],
        out_specs=pl.BlockSpec((None, PACK_ROWS, _H_WIDTH), lambda l, i: (l, i, 0)),
        out_shape=jax.ShapeDtypeStruct((depth, rows, _H_WIDTH), BF16),
        compiler_params=pltpu.CompilerParams(
            dimension_semantics=("arbitrary", "arbitrary"), vmem_limit_bytes=VMEM_LIMIT_BYTES),
        name="pack_w_in",
    )(padded)


def _pad_cols(w):
    return jnp.pad(w, ((0, 0), (0, 0), (0, PAD_COLS))).astype(BF16)


def kernel(x, norm_gains, ffn1_w_in, ffn1_w_out, w_in, conv_w, conv_b, i_bias, f_bias, m_norm_g, g_norm_g, g_norm_b, w_s, b_s, p_a, p_b, w_out, ffn2_w_in, ffn2_w_out):
    b, s, d = x.shape
    w = dict(
        h=_pack_w_in(w_in),
        p=_pad_cols(jnp.concatenate([p_a, p_b, w_out], axis=-1)),
        conv_w=conv_w,
        conv_b=conv_b[:, None, :],
        if_bias=jnp.pad(jnp.concatenate([i_bias, f_bias], axis=-1),
                        ((0, 0), (0, GATE_LANES - 2 * M_HEADS)))[:, None, :],
        m_norm_g=m_norm_g[:, None, :],
        g_norm_g=g_norm_g[:, None, :],
        g_norm_b=g_norm_b[:, None, :],
        w_s=w_s,
        b_s_t=jnp.swapaxes(b_s, 1, 2),
    )
    f1_in, f1_out = ffn1_w_in, ffn1_w_out
    f2_in, f2_out = ffn2_w_in, ffn2_w_out

    x2d = x.reshape(b * s, d)
    for layer in range(DEPTH):
        x2d = _ffn(x2d, norm_gains, f1_in, f1_out, layer, 0, 1)
        x2d = _mixer(x2d, norm_gains, w, layer, s // MIX_TM)
        x2d = _ffn(x2d, norm_gains, f2_in, f2_out, layer, 4, 5)
    return x2d.reshape(b, s, d)
```

```python
import functools

import jax
import jax.numpy as jnp
from jax import lax
from jax.experimental import pallas as pl
from jax.experimental.pallas import tpu as pltpu

D_MODEL = 1024
DEPTH = 4
M_HEADS = 4
M_DQK = 128
M_DV = 256
M_QK = M_HEADS * M_DQK
M_V = M_HEADS * M_DV
CHUNK = 128
CONV_W = 4
G_GROUPS = 4
G_WIDTH = 1024
G_DG = G_WIDTH // G_GROUPS
D_FF = 2816
EPS = 1e-6
GATE_LANES = 128
SUBLANES = 8
MXU_COLS = 256

FFN_TM = 512
FFN_ROWS = 256
FFN_CAST_ROWS = 128
MIX_TM = 256
VMEM_LIMIT_BYTES = 56 * 1024 * 1024

BF16 = jnp.bfloat16
F32 = jnp.float32


def _dot(a, b):
    return jnp.dot(a, b, preferred_element_type=F32)


def _rms(x, g):
    return x * lax.rsqrt(jnp.mean(x * x, axis=-1, keepdims=True) + EPS) * g


def _ln(x, g):
    mu = jnp.mean(x, axis=-1, keepdims=True)
    xc = x - mu
    var = jnp.mean(xc * xc, axis=-1, keepdims=True)
    return xc * lax.rsqrt(var + EPS) * g


def _log_sigmoid(x):
    return jnp.minimum(x, 0.0) - jnp.log1p(jnp.exp(-jnp.abs(x)))


def _cast_stream(t, src_hbm, dst_hbm, stage_f32, stage_bf16, sem_in, sem_out, first_step):
    chunk = stage_f32.shape[1]
    n = src_hbm.shape[0] // chunk
    k = t - first_step

    def rows(kk):
        return pl.ds(pl.multiple_of(kk * chunk, chunk), chunk)

    def fetch(kk):
        return pltpu.make_async_copy(src_hbm.at[rows(kk), :], stage_f32.at[kk % 2],
                                     sem_in.at[kk % 2])

    def write_back(kk):
        return pltpu.make_async_copy(stage_bf16.at[kk % 2], dst_hbm.at[rows(kk), :],
                                     sem_out.at[kk % 2])

    @pl.when(k == 0)
    def _():
        fetch(k).start()

    @pl.when(jnp.logical_and(k >= 0, k < n))
    def _():
        @pl.when(k + 1 < n)
        def _():
            fetch(k + 1).start()

        fetch(k).wait()

        @pl.when(k >= 2)
        def _():
            write_back(k - 2).wait()

        stage_bf16[k % 2] = stage_f32[k % 2].astype(BF16)
        write_back(k).start()

    @pl.when(k == n - 1)
    def _():
        write_back(k - 1).wait()
        write_back(k).wait()


def _ffn_body(x_ref, gains_ref, w_in_ref, w_out_ref, o_ref, gi, go):
    for r in range(0, FFN_TM, FFN_ROWS):
        rows = slice(r, r + FFN_ROWS)
        x = x_ref[rows, :]
        h = _rms(x, gains_ref[gi:gi + 1, :]).astype(BF16)
        a = _dot(h, w_in_ref[:, :D_FF])
        g = _dot(h, w_in_ref[:, D_FF:])
        s = (jax.nn.silu(a) * g).astype(BF16)
        y = _dot(s, w_out_ref[...])
        o_ref[rows, :] = x + 0.5 * _rms(y, gains_ref[go:go + 1, :])


def _ffn_kernel(x_ref, gains_ref, w_in_ref, w_out_ref, o_ref, *, gi, go):
    _ffn_body(x_ref, gains_ref, w_in_ref, w_out_ref, o_ref, gi, go)


def _ffn_prep_kernel(x_ref, gains_ref, w_in_ref, w_out_ref, nxt_in_hbm, nxt_out_hbm,
                     o_ref, cast_in_hbm, cast_out_hbm,
                     in_f32, in_bf16, out_f32, out_bf16, sems, *, gi, go, nxt_layer):
    t = pl.program_id(0)
    n_in = D_MODEL // FFN_CAST_ROWS
    _cast_stream(t, nxt_in_hbm.at[nxt_layer], cast_in_hbm, in_f32, in_bf16,
                 sems.at[0], sems.at[1], 0)
    _cast_stream(t, nxt_out_hbm.at[nxt_layer], cast_out_hbm, out_f32, out_bf16,
                 sems.at[2], sems.at[3], n_in)
    _ffn_body(x_ref, gains_ref, w_in_ref, w_out_ref, o_ref, gi, go)


def _resident(shape, index_map):
    return pl.BlockSpec(shape, index_map, pipeline_mode=pl.Buffered(1))


def _ffn(x2d, gains, w_in, w_out, layer, gi, go, nxt=None):
    n_steps = x2d.shape[0] // FFN_TM
    in_specs = [
        pl.BlockSpec((FFN_TM, D_MODEL), lambda i: (i, 0)),
        _resident((None, 6, D_MODEL), lambda i: (layer, 0, 0)),
        _resident((D_MODEL, 2 * D_FF), lambda i: (0, 0)),
        _resident((D_FF, D_MODEL), lambda i: (0, 0)),
    ]
    x_spec = pl.BlockSpec((FFN_TM, D_MODEL), lambda i: (i, 0))
    x_shape = jax.ShapeDtypeStruct(x2d.shape, F32)
    params = pltpu.CompilerParams(
        dimension_semantics=("arbitrary",), vmem_limit_bytes=VMEM_LIMIT_BYTES)
    if nxt is None:
        out = pl.pallas_call(
            functools.partial(_ffn_kernel, gi=gi, go=go),
            grid=(n_steps,), in_specs=in_specs, out_specs=x_spec, out_shape=x_shape,
            compiler_params=params, name="ffn",
        )(x2d, gains, w_in, w_out)
        return out, None, None
    nxt_in, nxt_out, nxt_layer = nxt
    assert (D_MODEL + D_FF) // FFN_CAST_ROWS <= n_steps
    hbm = pl.BlockSpec(memory_space=pl.ANY)
    return pl.pallas_call(
        functools.partial(_ffn_prep_kernel, gi=gi, go=go, nxt_layer=nxt_layer),
        grid=(n_steps,),
        in_specs=in_specs + [hbm, hbm],
        out_specs=[x_spec, hbm, hbm],
        out_shape=[x_shape, jax.ShapeDtypeStruct((D_MODEL, 2 * D_FF), BF16),
                   jax.ShapeDtypeStruct((D_FF, D_MODEL), BF16)],
        scratch_shapes=[
            pltpu.VMEM((2, FFN_CAST_ROWS, 2 * D_FF), F32),
            pltpu.VMEM((2, FFN_CAST_ROWS, 2 * D_FF), BF16),
            pltpu.VMEM((2, FFN_CAST_ROWS, D_MODEL), F32),
            pltpu.VMEM((2, FFN_CAST_ROWS, D_MODEL), BF16),
            pltpu.SemaphoreType.DMA((4, 2)),
        ],
        compiler_params=params, name="ffn_prep",
    )(x2d, gains, w_in, w_out, nxt_in, nxt_out)


_ACT_SHAPES = (
    ("qk", 2 * M_QK, F32), ("if", GATE_LANES, F32), ("v", M_V, BF16), ("o", M_V, F32),
    ("u", G_WIDTH, F32), ("vg", G_WIDTH, F32), ("ga", D_MODEL, F32), ("gb", D_MODEL, F32),
)
_N_ACT = len(_ACT_SHAPES)
_MIX_WEIGHTS = ("h", "p", "conv_w", "conv_b", "if_bias",
                "m_norm_g", "g_norm_g", "g_norm_b", "w_s", "b_s_t")

LANES = 128
PAD_COLS = LANES


def _col_ranges(names_widths):
    out, off = {}, 0
    for name, width in names_widths:
        out[name] = (off, off + width)
        off += width
    return out, off


_H_COLS, _H_WIDTH = _col_ranges((("qk", 2 * M_QK), ("v", M_V), ("if", GATE_LANES), ("o", M_V),
                                 ("u", G_WIDTH), ("vg", G_WIDTH), ("ga", D_MODEL), ("gb", D_MODEL)))
_P_COLS, _P_WIDTH = _col_ranges((("p_a", D_MODEL), ("p_b", D_MODEL), ("w_out", D_MODEL),
                                 ("pad", PAD_COLS)))


def _chunk_cumsum(v, row_in_chunk):
    shift = 1
    while shift < CHUNK:
        rolled = pltpu.roll(v, shift, axis=0)
        v = v + jnp.where(row_in_chunk >= shift, rolled, 0.0)
        shift *= 2
    return v


def _shift_rows(cur, tail, s, row_in_vreg):
    rolled = pltpu.roll(cur, s, axis=0)
    head = jnp.where(row_in_vreg < s, pltpu.roll(tail, s, axis=0), rolled[0:SUBLANES])
    return jnp.concatenate([head, rolled[SUBLANES:]], axis=0)


def _mixer_step(xn_ref, xb_ref, gains_ref, w, o_ref, act_w, act_r, h_cur_ref, h_next_ref,
                tail_ref, ct_ref, n_ref, m_ref, ya_ref, hg_ref, q_ref, k_ref, gates_ref,
                og_ref, u_ref, vln_ref):
    tm = MIX_TM
    nch = tm // CHUNK

    tri_r = lax.broadcasted_iota(jnp.int32, (CHUNK, CHUNK), 0)
    tri_c = lax.broadcasted_iota(jnp.int32, (CHUNK, CHUNK), 1)
    causal = tri_r >= tri_c

    def mlstm_unit(c, hd):
        rows = slice(c * CHUNK, (c + 1) * CHUNK)
        qk_cols = slice(hd * M_DQK, (hd + 1) * M_DQK)
        v_cols = slice(hd * M_DV, (hd + 1) * M_DV)
        gc = gates_ref[rows, :]
        gct = gc.T
        q = q_ref[rows, qk_cols]
        k = k_ref[rows, qk_cols]
        v = act_r["v"][rows, v_cols]
        i_col = gc[:, hd:hd + 1]
        b_col = gc[:, M_HEADS + hd:M_HEADS + hd + 1]
        i_row = gct[hd:hd + 1, :]
        b_row = gct[M_HEADS + hd:M_HEADS + hd + 1, :]
        b_last = b_row[:, CHUNK - 1:CHUNK]
        m_prev = m_ref[hd][:, 0:1]
        ct = ct_ref[hd]
        n_row = n_ref[hd]

        a_row = i_row - b_row
        d = jnp.where(causal, b_col + a_row, -jnp.inf)
        inter = b_col + m_prev
        m_t = jnp.maximum(inter, jnp.max(d, axis=-1, keepdims=True))
        s = lax.dot_general(q.astype(BF16), k.astype(BF16), (((1,), (1,)), ((), ())),
                            preferred_element_type=F32)
        p = jnp.exp(d - m_t) * s
        s_inter = jnp.exp(inter - m_t)
        lhs = jnp.concatenate([p, s_inter * q], axis=1).astype(BF16)
        rhs = jnp.concatenate([v, ct.astype(BF16)], axis=0)
        num = _dot(lhs, rhs)
        den = (jnp.sum(p, axis=-1, keepdims=True)
               + s_inter * jnp.sum(q * n_row, axis=-1, keepdims=True))
        hh = num / jnp.maximum(jnp.abs(den), jnp.exp(-m_t))

        g_row = b_last + a_row
        m_new = jnp.maximum(b_last + m_prev, jnp.max(g_row, axis=-1, keepdims=True))
        w_col = jnp.exp(b_last - b_col + i_col - m_new)
        decay = jnp.exp(b_last + m_prev - m_new)
        kw = w_col * k
        ct_ref[hd] = decay * ct + lax.dot_general(
            kw.astype(BF16), v, (((0,), (0,)), ((), ())), preferred_element_type=F32)
        n_ref[hd] = decay * n_row + jnp.sum(kw, axis=0, keepdims=True)
        m_ref[hd] = jnp.broadcast_to(m_new, (1, GATE_LANES))

        hm = _ln(hh, w["m_norm_g"][:, v_cols])
        ya_ref[rows, v_cols] = (og_ref[rows, v_cols] * hm).astype(BF16)

    def gmlp_group(grp):
        w_tri = jnp.where(causal, w["w_s"][grp], 0.0).astype(BF16)
        bias_col = w["b_s_t"][:, grp:grp + 1]
        cols = slice(grp * G_DG, (grp + 1) * G_DG)
        u_ref[:, cols] = jax.nn.gelu(act_r["u"][:, cols])
        for c in range(nch):
            rows = slice(c * CHUNK, (c + 1) * CHUNK)
            sg = _dot(w_tri, vln_ref[rows, cols]) + bias_col
            hg_ref[rows, cols] = (u_ref[rows, cols] * sg).astype(BF16)

    def wp(name):
        lo, hi = _P_COLS[name]
        return w["p"][:, lo:hi]

    def proj_block(name, lo, hi):
        base = _H_COLS[name][0]
        z = _dot(h_cur_ref[...], w["h"][:, base + lo:base + hi])
        act_w[name][:, lo:hi] = z.astype(act_w[name].dtype)

    def gates():
        pre = act_r["if"][...] + w["if_bias"][...]
        col = lax.broadcasted_iota(jnp.int32, (tm, GATE_LANES), 1)
        row_in_chunk = lax.broadcasted_iota(jnp.int32, (tm, GATE_LANES), 0) % CHUNK
        gates_ref[...] = jnp.where(
            col >= M_HEADS, _chunk_cumsum(_log_sigmoid(pre), row_in_chunk), pre)

    def conv_block(lo, hi):
        zqk = act_r["qk"][:, lo:hi]
        row_in_vreg = lax.broadcasted_iota(jnp.int32, (SUBLANES, hi - lo), 0)
        tail = tail_ref[:, lo:hi]
        conv = w["conv_b"][:, lo:hi] + w["conv_w"][CONV_W - 1:CONV_W, lo:hi] * zqk
        for s in range(1, CONV_W):
            j = CONV_W - 1 - s
            conv = conv + w["conv_w"][j:j + 1, lo:hi] * _shift_rows(zqk, tail, s, row_in_vreg)
        tail_ref[:, lo:hi] = zqk[tm - SUBLANES:tm, :]
        y = jax.nn.silu(conv)
        if lo < M_QK:
            q_ref[:, lo:hi] = y * (M_DQK ** -0.5)
        else:
            k_ref[:, lo - M_QK:hi - M_QK] = y

    def out_gate(hd):
        v_cols = slice(hd * M_DV, (hd + 1) * M_DV)
        og_ref[:, v_cols] = jax.nn.sigmoid(act_r["o"][:, v_cols])

    def gmlp_norm():
        vln_ref[...] = (_ln(jax.nn.gelu(act_r["vg"][...]), w["g_norm_g"][...])
                        + w["g_norm_b"][...]).astype(BF16)

    proj = []
    for name, width, _ in _ACT_SHAPES:
        for lo in range(0, width, MXU_COLS):
            hi = min(lo + MXU_COLS, width)
            proj.append((hi - lo, functools.partial(proj_block, name, lo, hi)))
    rec = [(300, gates)]
    rec += [(350, functools.partial(conv_block, lo, lo + MXU_COLS))
            for lo in range(0, 2 * M_QK, MXU_COLS)]
    rec += [(100, functools.partial(out_gate, hd)) for hd in range(M_HEADS)]
    rec += [(450, functools.partial(mlstm_unit, 0, hd)) for hd in range(M_HEADS)]
    rec += [(900, gmlp_norm)]
    rec += [(300, functools.partial(gmlp_group, grp)) for grp in range(G_GROUPS)]
    rec += [(450, functools.partial(mlstm_unit, c, hd))
            for c in range(1, nch) for hd in range(M_HEADS)]
    proj_total = sum(c for c, _ in proj)
    rec_total = sum(c for c, _ in rec)
    done_proj, done_rec, ip = 0, 0, 0
    for cost, run in rec:
        while ip < len(proj) and done_proj * rec_total <= done_rec * proj_total:
            done_proj += proj[ip][0]
            proj[ip][1]()
            ip += 1
        run()
        done_rec += cost
    for _, run in proj[ip:]:
        run()

    h_next_ref[...] = _rms(xn_ref[...], gains_ref[2:3, :]).astype(BF16)
    y_b = _dot(hg_ref[...], wp("p_b"))
    y_a = _dot(ya_ref[...], wp("p_a"))
    merged = jax.nn.sigmoid(act_r["ga"][...]) * y_a + jax.nn.sigmoid(act_r["gb"][...]) * y_b
    mix = _dot(merged.astype(BF16), wp("w_out"))
    o_ref[...] = xb_ref[...] + _rms(mix, gains_ref[3:4, :])


def _mixer_kernel(*refs, tiles_per_row):
    xn_ref, xb_ref, gains_ref = refs[:3]
    nw = len(_MIX_WEIGHTS)
    w = dict(zip(_MIX_WEIGHTS, refs[3:3 + nw]))
    o_ref = refs[3 + nw]
    scratch = refs[4 + nw:]
    act = [dict(zip((n for n, _, _ in _ACT_SHAPES), scratch[i * _N_ACT:(i + 1) * _N_ACT]))
           for i in range(2)]
    h_refs = scratch[2 * _N_ACT:2 * _N_ACT + 2]
    rest = scratch[2 * _N_ACT + 2:]
    tail_ref, ct_ref, n_ref, m_ref = rest[:4]
    t = pl.program_id(0)

    @pl.when(t == 0)
    def _():
        for ref in act[1].values():
            ref[...] = jnp.zeros_like(ref)
        h_refs[0][...] = _rms(xb_ref[...], gains_ref[2:3, :]).astype(BF16)

    @pl.when(jnp.logical_or(t == 0, (t + tiles_per_row - 1) % tiles_per_row == 0))
    def _():
        tail_ref[...] = jnp.zeros_like(tail_ref)
        ct_ref[...] = jnp.zeros_like(ct_ref)
        n_ref[...] = jnp.zeros_like(n_ref)
        m_ref[...] = jnp.zeros_like(m_ref)

    step = functools.partial(_mixer_step, xn_ref, xb_ref, gains_ref, w, o_ref)

    @pl.when(t % 2 == 0)
    def _():
        step(act[0], act[1], h_refs[0], h_refs[1], *rest)

    @pl.when(t % 2 == 1)
    def _():
        step(act[1], act[0], h_refs[1], h_refs[0], *rest)


def _mixer(x2d, gains, w, layer, tiles_per_row):
    tm = MIX_TM
    n_tiles = x2d.shape[0] // tm

    def lay(shape):
        nd = len(shape)
        return _resident((None,) + shape, lambda t: (layer,) + (0,) * nd)

    w_specs = {
        "h": lay((D_MODEL, _H_WIDTH)), "p": lay((D_MODEL, _P_WIDTH)),
        "conv_w": lay((CONV_W, 2 * M_QK)), "conv_b": lay((1, 2 * M_QK)),
        "if_bias": lay((1, GATE_LANES)), "m_norm_g": lay((1, M_V)),
        "g_norm_g": lay((1, G_WIDTH)), "g_norm_b": lay((1, G_WIDTH)),
        "w_s": lay((G_GROUPS, CHUNK, CHUNK)), "b_s_t": lay((CHUNK, G_GROUPS)),
    }
    in_specs = [
        pl.BlockSpec((tm, D_MODEL), lambda t: (jnp.minimum(t + 1, n_tiles - 1), 0)),
        pl.BlockSpec((tm, D_MODEL), lambda t: (jnp.maximum(t - 1, 0), 0)),
        lay((6, D_MODEL)),
    ] + [w_specs[name] for name in _MIX_WEIGHTS]
    act_scratch = [pltpu.VMEM((tm, width), dt) for _, width, dt in _ACT_SHAPES]
    return pl.pallas_call(
        functools.partial(_mixer_kernel, tiles_per_row=tiles_per_row),
        grid=(n_tiles + 1,),
        in_specs=in_specs,
        out_specs=pl.BlockSpec((tm, D_MODEL), lambda t: (jnp.maximum(t - 1, 0), 0)),
        out_shape=jax.ShapeDtypeStruct(x2d.shape, F32),
        scratch_shapes=act_scratch + act_scratch + [
            pltpu.VMEM((tm, D_MODEL), BF16),
            pltpu.VMEM((tm, D_MODEL), BF16),
            pltpu.VMEM((SUBLANES, 2 * M_QK), F32),
            pltpu.VMEM((M_HEADS, M_DQK, M_DV), F32),
            pltpu.VMEM((M_HEADS, 1, M_DQK), F32),
            pltpu.VMEM((M_HEADS, 1, GATE_LANES), F32),
            pltpu.VMEM((tm, M_V), BF16),
            pltpu.VMEM((tm, G_WIDTH), BF16),
            pltpu.VMEM((tm, M_QK), F32),
            pltpu.VMEM((tm, M_QK), F32),
            pltpu.VMEM((tm, GATE_LANES), F32),
            pltpu.VMEM((tm, M_V), F32),
            pltpu.VMEM((tm, G_WIDTH), F32),
            pltpu.VMEM((tm, G_WIDTH), BF16),
        ],
        compiler_params=pltpu.CompilerParams(
            dimension_semantics=("arbitrary",), vmem_limit_bytes=VMEM_LIMIT_BYTES),
        name="mixer",
    )(x2d, x2d, gains, *[w[name] for name in _MIX_WEIGHTS])


PACK_ROWS = 256


def _pack_w_in_kernel(w_ref, o_ref):
    gate_lo = 2 * M_QK + M_V
    n_gate = 2 * M_HEADS
    o_ref[:, :gate_lo] = w_ref[:, :gate_lo]
    lane = lax.broadcasted_iota(jnp.int32, (PACK_ROWS, GATE_LANES), 1)
    o_ref[:, gate_lo:gate_lo + GATE_LANES] = jnp.where(
        lane < n_gate, w_ref[:, gate_lo:gate_lo + GATE_LANES], jnp.zeros((), BF16))
    rest = _H_WIDTH - gate_lo - GATE_LANES
    o_ref[:, gate_lo + GATE_LANES:] = w_ref[:, gate_lo:][:, n_gate:n_gate + rest]


def _pack_w_in(w_in):
    depth, rows, cols = w_in.shape
    padded = jnp.pad(w_in, ((0, 0), (0, 0), (0, _H_WIDTH - cols))).astype(BF16)
    return pl.pallas_call(
        _pack_w_in_kernel,
        grid=(depth, rows // PACK_ROWS),
        in_specs=[pl.BlockSpec((None, PACK_ROWS, _H_WIDTH), lambda l, i: (l, i, 0))],
        out_specs=pl.BlockSpec((None, PACK_ROWS, _H_WIDTH), lambda l, i: (l, i, 0)),
        out_shape=jax.ShapeDtypeStruct((depth, rows, _H_WIDTH), BF16),
        compiler_params=pltpu.CompilerParams(
            dimension_semantics=("arbitrary", "arbitrary"), vmem_limit_bytes=VMEM_LIMIT_BYTES),
        name="pack_w_in",
    )(padded)


def _pad_cols(w):
    return jnp.pad(w, ((0, 0), (0, 0), (0, PAD_COLS))).astype(BF16)


def kernel(x, norm_gains, ffn1_w_in, ffn1_w_out, w_in, conv_w, conv_b, i_bias, f_bias, m_norm_g, g_norm_g, g_norm_b, w_s, b_s, p_a, p_b, w_out, ffn2_w_in, ffn2_w_out):
    b, s, d = x.shape
    w = dict(
        h=_pack_w_in(w_in),
        p=_pad_cols(jnp.concatenate([p_a, p_b, w_out], axis=-1)),
        conv_w=conv_w,
        conv_b=conv_b[:, None, :],
        if_bias=jnp.pad(jnp.concatenate([i_bias, f_bias], axis=-1),
                        ((0, 0), (0, GATE_LANES - 2 * M_HEADS)))[:, None, :],
        m_norm_g=m_norm_g[:, None, :],
        g_norm_g=g_norm_g[:, None, :],
        g_norm_b=g_norm_b[:, None, :],
        w_s=w_s,
        b_s_t=jnp.swapaxes(b_s, 1, 2),
    )
    ffn_w = [(ffn1_w_in, ffn1_w_out), (ffn2_w_in, ffn2_w_out)]
    w_in_bf, w_out_bf = ffn1_w_in[0].astype(BF16), ffn1_w_out[0].astype(BF16)

    x2d = x.reshape(b * s, d)
    for layer in range(DEPTH):
        for which, (gi, go) in enumerate(((0, 1), (4, 5))):
            call = 2 * layer + which
            nxt = None
            if call + 1 < 2 * DEPTH:
                nxt = ffn_w[(call + 1) % 2] + ((call + 1) // 2,)
            x2d, w_in_bf, w_out_bf = _ffn(x2d, norm_gains, w_in_bf, w_out_bf, layer, gi, go, nxt)
            if which == 0:
                x2d = _mixer(x2d, norm_gains, w, layer, s // MIX_TM)
    return x2d.reshape(b, s, d)
```

```python
import functools

import jax
import jax.numpy as jnp
from jax import lax
from jax.experimental import pallas as pl
from jax.experimental.pallas import tpu as pltpu

D_MODEL = 1024
DEPTH = 4
M_HEADS = 4
M_DQK = 128
M_DV = 256
M_QK = M_HEADS * M_DQK
M_V = M_HEADS * M_DV
CHUNK = 128
CONV_W = 4
G_GROUPS = 4
G_WIDTH = 1024
G_DG = G_WIDTH // G_GROUPS
D_FF = 2816
EPS = 1e-6
GATE_LANES = 128
SUBLANES = 8
MXU_COLS = 256

FFN_TM = 512
FFN_ROWS = 256
FFN_CAST_ROWS = 128
MIX_TM = 256
VMEM_LIMIT_BYTES = 56 * 1024 * 1024

BF16 = jnp.bfloat16
F32 = jnp.float32


def _dot(a, b):
    return jnp.dot(a, b, preferred_element_type=F32)


def _rms(x, g):
    return x * lax.rsqrt(jnp.mean(x * x, axis=-1, keepdims=True) + EPS) * g


def _ln(x, g):
    mu = jnp.mean(x, axis=-1, keepdims=True)
    xc = x - mu
    var = jnp.mean(xc * xc, axis=-1, keepdims=True)
    return xc * lax.rsqrt(var + EPS) * g


def _log_sigmoid(x):
    return jnp.minimum(x, 0.0) - jnp.log1p(jnp.exp(-jnp.abs(x)))


def _cast_stream(t, src_hbm, dst_hbm, stage_f32, stage_bf16, sem_in, sem_out, first_step):
    chunk = stage_f32.shape[1]
    n = src_hbm.shape[0] // chunk
    k = t - first_step

    def rows(kk):
        return pl.ds(pl.multiple_of(kk * chunk, chunk), chunk)

    def fetch(kk):
        return pltpu.make_async_copy(src_hbm.at[rows(kk), :], stage_f32.at[kk % 2],
                                     sem_in.at[kk % 2])

    def write_back(kk):
        return pltpu.make_async_copy(stage_bf16.at[kk % 2], dst_hbm.at[rows(kk), :],
                                     sem_out.at[kk % 2])

    @pl.when(k == 0)
    def _():
        fetch(k).start()

    @pl.when(jnp.logical_and(k >= 0, k < n))
    def _():
        @pl.when(k + 1 < n)
        def _():
            fetch(k + 1).start()

        fetch(k).wait()

        @pl.when(k >= 2)
        def _():
            write_back(k - 2).wait()

        stage_bf16[k % 2] = stage_f32[k % 2].astype(BF16)
        write_back(k).start()

    @pl.when(k == n - 1)
    def _():
        write_back(k - 1).wait()
        write_back(k).wait()


def _ffn_body(x_ref, gains_ref, w_in_ref, w_out_ref, o_ref, gi, go):
    for r in range(0, FFN_TM, FFN_ROWS):
        rows = slice(r, r + FFN_ROWS)
        x = x_ref[rows, :]
        h = _rms(x, gains_ref[gi:gi + 1, :]).astype(BF16)
        a = _dot(h, w_in_ref[:, :D_FF])
        g = _dot(h, w_in_ref[:, D_FF:])
        s = (jax.nn.silu(a) * g).astype(BF16)
        y = _dot(s, w_out_ref[...])
        o_ref[rows, :] = x + 0.5 * _rms(y, gains_ref[go:go + 1, :])


def _ffn_kernel(x_ref, gains_ref, w_in_ref, w_out_ref, o_ref, *, gi, go):
    _ffn_body(x_ref, gains_ref, w_in_ref, w_out_ref, o_ref, gi, go)


def _ffn_prep_kernel(x_ref, gains_ref, w_in_ref, w_out_ref, nxt_in_hbm, nxt_out_hbm,
                     o_ref, cast_in_hbm, cast_out_hbm,
                     in_f32, in_bf16, out_f32, out_bf16, sem_a, sem_b, sem_c, sem_d,
                     *, gi, go, nxt_layer):
    t = pl.program_id(0)
    n_in = D_MODEL // FFN_CAST_ROWS
    _cast_stream(t, nxt_in_hbm.at[nxt_layer], cast_in_hbm, in_f32, in_bf16,
                 sem_a, sem_b, 0)
    _cast_stream(t, nxt_out_hbm.at[nxt_layer], cast_out_hbm, out_f32, out_bf16,
                 sem_c, sem_d, n_in)
    _ffn_body(x_ref, gains_ref, w_in_ref, w_out_ref, o_ref, gi, go)


def _resident(shape, index_map):
    return pl.BlockSpec(shape, index_map, pipeline_mode=pl.Buffered(1))


def _ffn(x2d, gains, w_in, w_out, layer, gi, go, nxt=None):
    n_steps = x2d.shape[0] // FFN_TM
    in_specs = [
        pl.BlockSpec((FFN_TM, D_MODEL), lambda i: (i, 0)),
        _resident((None, 6, D_MODEL), lambda i: (layer, 0, 0)),
        _resident((D_MODEL, 2 * D_FF), lambda i: (0, 0)),
        _resident((D_FF, D_MODEL), lambda i: (0, 0)),
    ]
    x_spec = pl.BlockSpec((FFN_TM, D_MODEL), lambda i: (i, 0))
    x_shape = jax.ShapeDtypeStruct(x2d.shape, F32)
    params = pltpu.CompilerParams(
        dimension_semantics=("arbitrary",), vmem_limit_bytes=VMEM_LIMIT_BYTES)
    if nxt is None:
        out = pl.pallas_call(
            functools.partial(_ffn_kernel, gi=gi, go=go),
            grid=(n_steps,), in_specs=in_specs, out_specs=x_spec, out_shape=x_shape,
            compiler_params=params, name="ffn",
        )(x2d, gains, w_in, w_out)
        return out, None, None
    nxt_in, nxt_out, nxt_layer = nxt
    assert (D_MODEL + D_FF) // FFN_CAST_ROWS <= n_steps
    hbm = pl.BlockSpec(memory_space=pl.ANY)
    return pl.pallas_call(
        functools.partial(_ffn_prep_kernel, gi=gi, go=go, nxt_layer=nxt_layer),
        grid=(n_steps,),
        in_specs=in_specs + [hbm, hbm],
        out_specs=[x_spec, hbm, hbm],
        out_shape=[x_shape, jax.ShapeDtypeStruct((D_MODEL, 2 * D_FF), BF16),
                   jax.ShapeDtypeStruct((D_FF, D_MODEL), BF16)],
        scratch_shapes=[
            pltpu.VMEM((2, FFN_CAST_ROWS, 2 * D_FF), F32),
            pltpu.VMEM((2, FFN_CAST_ROWS, 2 * D_FF), BF16),
            pltpu.VMEM((2, FFN_CAST_ROWS, D_MODEL), F32),
            pltpu.VMEM((2, FFN_CAST_ROWS, D_MODEL), BF16),
            pltpu.SemaphoreType.DMA((2,)), pltpu.SemaphoreType.DMA((2,)),
            pltpu.SemaphoreType.DMA((2,)), pltpu.SemaphoreType.DMA((2,)),
        ],
        compiler_params=params, name="ffn_prep",
    )(x2d, gains, w_in, w_out, nxt_in, nxt_out)


_ACT_SHAPES = (
    ("qk", 2 * M_QK, F32), ("if", GATE_LANES, F32), ("v", M_V, BF16), ("o", M_V, F32),
    ("u", G_WIDTH, F32), ("vg", G_WIDTH, F32), ("ga", D_MODEL, F32), ("gb", D_MODEL, F32),
)
_N_ACT = len(_ACT_SHAPES)
_MIX_WEIGHTS = ("h", "p", "conv_w", "conv_b", "if_bias",
                "m_norm_g", "g_norm_g", "g_norm_b", "w_s", "b_s_t")

LANES = 128
PAD_COLS = LANES


def _col_ranges(names_widths):
    out, off = {}, 0
    for name, width in names_widths:
        out[name] = (off, off + width)
        off += width
    return out, off


_H_COLS, _H_WIDTH = _col_ranges((("qk", 2 * M_QK), ("v", M_V), ("if", GATE_LANES), ("o", M_V),
                                 ("u", G_WIDTH), ("vg", G_WIDTH), ("ga", D_MODEL), ("gb", D_MODEL)))
_P_COLS, _P_WIDTH = _col_ranges((("p_a", D_MODEL), ("p_b", D_MODEL), ("w_out", D_MODEL),
                                 ("pad", PAD_COLS)))


def _chunk_cumsum(v, row_in_chunk):
    shift = 1
    while shift < CHUNK:
        rolled = pltpu.roll(v, shift, axis=0)
        v = v + jnp.where(row_in_chunk >= shift, rolled, 0.0)
        shift *= 2
    return v


def _shift_rows(cur, tail, s, row_in_vreg):
    rolled = pltpu.roll(cur, s, axis=0)
    head = jnp.where(row_in_vreg < s, pltpu.roll(tail, s, axis=0), rolled[0:SUBLANES])
    return jnp.concatenate([head, rolled[SUBLANES:]], axis=0)


def _mixer_step(xn_ref, xb_ref, gains_ref, w, o_ref, act_w, act_r, h_cur_ref, h_next_ref,
                tail_ref, ct_ref, n_ref, m_ref, ya_ref, hg_ref, q_ref, k_ref, gates_ref,
                og_ref, u_ref, vln_ref):
    tm = MIX_TM
    nch = tm // CHUNK

    tri_r = lax.broadcasted_iota(jnp.int32, (CHUNK, CHUNK), 0)
    tri_c = lax.broadcasted_iota(jnp.int32, (CHUNK, CHUNK), 1)
    causal = tri_r >= tri_c

    def mlstm_unit(c, hd):
        rows = slice(c * CHUNK, (c + 1) * CHUNK)
        qk_cols = slice(hd * M_DQK, (hd + 1) * M_DQK)
        v_cols = slice(hd * M_DV, (hd + 1) * M_DV)
        gc = gates_ref[rows, :]
        gct = gc.T
        q = q_ref[rows, qk_cols]
        k = k_ref[rows, qk_cols]
        v = act_r["v"][rows, v_cols]
        i_col = gc[:, hd:hd + 1]
        b_col = gc[:, M_HEADS + hd:M_HEADS + hd + 1]
        i_row = gct[hd:hd + 1, :]
        b_row = gct[M_HEADS + hd:M_HEADS + hd + 1, :]
        b_last = b_row[:, CHUNK - 1:CHUNK]
        m_prev = m_ref[hd][:, 0:1]
        ct = ct_ref[hd]
        n_row = n_ref[hd]

        a_row = i_row - b_row
        d = jnp.where(causal, b_col + a_row, -jnp.inf)
        inter = b_col + m_prev
        m_t = jnp.maximum(inter, jnp.max(d, axis=-1, keepdims=True))
        s = lax.dot_general(q.astype(BF16), k.astype(BF16), (((1,), (1,)), ((), ())),
                            preferred_element_type=F32)
        p = jnp.exp(d - m_t) * s
        s_inter = jnp.exp(inter - m_t)
        lhs = jnp.concatenate([p, s_inter * q], axis=1).astype(BF16)
        rhs = jnp.concatenate([v, ct.astype(BF16)], axis=0)
        num = _dot(lhs, rhs)
        den = (jnp.sum(p, axis=-1, keepdims=True)
               + s_inter * jnp.sum(q * n_row, axis=-1, keepdims=True))
        hh = num / jnp.maximum(jnp.abs(den), jnp.exp(-m_t))

        g_row = b_last + a_row
        m_new = jnp.maximum(b_last + m_prev, jnp.max(g_row, axis=-1, keepdims=True))
        w_col = jnp.exp(b_last - b_col + i_col - m_new)
        decay = jnp.exp(b_last + m_prev - m_new)
        kw = w_col * k
        ct_ref[hd] = decay * ct + lax.dot_general(
            kw.astype(BF16), v, (((0,), (0,)), ((), ())), preferred_element_type=F32)
        n_ref[hd] = decay * n_row + jnp.sum(kw, axis=0, keepdims=True)
        m_ref[hd] = jnp.broadcast_to(m_new, (1, GATE_LANES))

        hm = _ln(hh, w["m_norm_g"][:, v_cols])
        ya_ref[rows, v_cols] = (og_ref[rows, v_cols] * hm).astype(BF16)

    def gmlp_group(grp):
        w_tri = jnp.where(causal, w["w_s"][grp], 0.0).astype(BF16)
        bias_col = w["b_s_t"][:, grp:grp + 1]
        cols = slice(grp * G_DG, (grp + 1) * G_DG)
        u_ref[:, cols] = jax.nn.gelu(act_r["u"][:, cols])
        for c in range(nch):
            rows = slice(c * CHUNK, (c + 1) * CHUNK)
            sg = _dot(w_tri, vln_ref[rows, cols]) + bias_col
            hg_ref[rows, cols] = (u_ref[rows, cols] * sg).astype(BF16)

    def wp(name):
        lo, hi = _P_COLS[name]
        return w["p"][:, lo:hi]

    def proj_block(name, lo, hi):
        base = _H_COLS[name][0]
        z = _dot(h_cur_ref[...], w["h"][:, base + lo:base + hi])
        act_w[name][:, lo:hi] = z.astype(act_w[name].dtype)

    def gates():
        pre = act_r["if"][...] + w["if_bias"][...]
        col = lax.broadcasted_iota(jnp.int32, (tm, GATE_LANES), 1)
        row_in_chunk = lax.broadcasted_iota(jnp.int32, (tm, GATE_LANES), 0) % CHUNK
        gates_ref[...] = jnp.where(
            col >= M_HEADS, _chunk_cumsum(_log_sigmoid(pre), row_in_chunk), pre)

    def conv_block(lo, hi):
        zqk = act_r["qk"][:, lo:hi]
        row_in_vreg = lax.broadcasted_iota(jnp.int32, (SUBLANES, hi - lo), 0)
        tail = tail_ref[:, lo:hi]
        conv = w["conv_b"][:, lo:hi] + w["conv_w"][CONV_W - 1:CONV_W, lo:hi] * zqk
        for s in range(1, CONV_W):
            j = CONV_W - 1 - s
            conv = conv + w["conv_w"][j:j + 1, lo:hi] * _shift_rows(zqk, tail, s, row_in_vreg)
        tail_ref[:, lo:hi] = zqk[tm - SUBLANES:tm, :]
        y = jax.nn.silu(conv)
        if lo < M_QK:
            q_ref[:, lo:hi] = y * (M_DQK ** -0.5)
        else:
            k_ref[:, lo - M_QK:hi - M_QK] = y

    def out_gate(hd):
        v_cols = slice(hd * M_DV, (hd + 1) * M_DV)
        og_ref[:, v_cols] = jax.nn.sigmoid(act_r["o"][:, v_cols])

    def gmlp_norm():
        vln_ref[...] = (_ln(jax.nn.gelu(act_r["vg"][...]), w["g_norm_g"][...])
                        + w["g_norm_b"][...]).astype(BF16)

    branch_b = {}

    def gmlp_out():
        branch_b["y"] = _dot(hg_ref[...], wp("p_b"))

    proj = []
    for name, width, _ in _ACT_SHAPES:
        for lo in range(0, width, MXU_COLS):
            hi = min(lo + MXU_COLS, width)
            proj.append((hi - lo, functools.partial(proj_block, name, lo, hi)))
    rec = [(300, gates)]
    rec += [(350, functools.partial(conv_block, lo, lo + MXU_COLS))
            for lo in range(0, 2 * M_QK, MXU_COLS)]
    rec += [(100, functools.partial(out_gate, hd)) for hd in range(M_HEADS)]
    rec += [(450, functools.partial(mlstm_unit, 0, hd)) for hd in range(M_HEADS)]
    rec += [(900, gmlp_norm)]
    late = [(450, functools.partial(mlstm_unit, c, hd))
            for c in range(1, nch) for hd in range(M_HEADS)]
    rec += late[:len(late) // 2]
    rec += [(300, functools.partial(gmlp_group, grp)) for grp in range(G_GROUPS)]
    rec += [(0, gmlp_out)]
    rec += late[len(late) // 2:]
    proj_total = sum(c for c, _ in proj)
    rec_total = sum(c for c, _ in rec)
    done_proj, done_rec, ip = 0, 0, 0
    for cost, run in rec:
        while ip < len(proj) and done_proj * rec_total <= done_rec * proj_total:
            done_proj += proj[ip][0]
            proj[ip][1]()
            ip += 1
        run()
        done_rec += cost
    for _, run in proj[ip:]:
        run()

    h_next_ref[...] = _rms(xn_ref[...], gains_ref[2:3, :]).astype(BF16)
    y_a = _dot(ya_ref[...], wp("p_a"))
    merged = (jax.nn.sigmoid(act_r["ga"][...]) * y_a
              + jax.nn.sigmoid(act_r["gb"][...]) * branch_b["y"])
    mix = _dot(merged.astype(BF16), wp("w_out"))
    o_ref[...] = xb_ref[...] + _rms(mix, gains_ref[3:4, :])


def _mixer_kernel(*refs, tiles_per_row):
    xn_ref, xb_ref, gains_ref = refs[:3]
    nw = len(_MIX_WEIGHTS)
    w = dict(zip(_MIX_WEIGHTS, refs[3:3 + nw]))
    o_ref = refs[3 + nw]
    scratch = refs[4 + nw:]
    act = [dict(zip((n for n, _, _ in _ACT_SHAPES), scratch[i * _N_ACT:(i + 1) * _N_ACT]))
           for i in range(2)]
    h_refs = scratch[2 * _N_ACT:2 * _N_ACT + 2]
    rest = scratch[2 * _N_ACT + 2:]
    tail_ref, ct_ref, n_ref, m_ref = rest[:4]
    t = pl.program_id(0)

    @pl.when(t == 0)
    def _():
        for ref in act[1].values():
            ref[...] = jnp.zeros_like(ref)
        h_refs[0][...] = _rms(xb_ref[...], gains_ref[2:3, :]).astype(BF16)

    @pl.when(jnp.logical_or(t == 0, (t + tiles_per_row - 1) % tiles_per_row == 0))
    def _():
        tail_ref[...] = jnp.zeros_like(tail_ref)
        ct_ref[...] = jnp.zeros_like(ct_ref)
        n_ref[...] = jnp.zeros_like(n_ref)
        m_ref[...] = jnp.zeros_like(m_ref)

    step = functools.partial(_mixer_step, xn_ref, xb_ref, gains_ref, w, o_ref)

    @pl.when(t % 2 == 0)
    def _():
        step(act[0], act[1], h_refs[0], h_refs[1], *rest)

    @pl.when(t % 2 == 1)
    def _():
        step(act[1], act[0], h_refs[1], h_refs[0], *rest)


def _mixer(x2d, gains, w, layer, tiles_per_row):
    tm = MIX_TM
    n_tiles = x2d.shape[0] // tm

    def lay(shape):
        nd = len(shape)
        return _resident((None,) + shape, lambda t: (layer,) + (0,) * nd)

    w_specs = {
        "h": lay((D_MODEL, _H_WIDTH)), "p": lay((D_MODEL, _P_WIDTH)),
        "conv_w": lay((CONV_W, 2 * M_QK)), "conv_b": lay((1, 2 * M_QK)),
        "if_bias": lay((1, GATE_LANES)), "m_norm_g": lay((1, M_V)),
        "g_norm_g": lay((1, G_WIDTH)), "g_norm_b": lay((1, G_WIDTH)),
        "w_s": lay((G_GROUPS, CHUNK, CHUNK)), "b_s_t": lay((CHUNK, G_GROUPS)),
    }
    in_specs = [
        pl.BlockSpec((tm, D_MODEL), lambda t: (jnp.minimum(t + 1, n_tiles - 1), 0)),
        pl.BlockSpec((tm, D_MODEL), lambda t: (jnp.maximum(t - 1, 0), 0)),
        lay((6, D_MODEL)),
    ] + [w_specs[name] for name in _MIX_WEIGHTS]
    act_scratch = [pltpu.VMEM((tm, width), dt) for _, width, dt in _ACT_SHAPES]
    return pl.pallas_call(
        functools.partial(_mixer_kernel, tiles_per_row=tiles_per_row),
        grid=(n_tiles + 1,),
        in_specs=in_specs,
        out_specs=pl.BlockSpec((tm, D_MODEL), lambda t: (jnp.maximum(t - 1, 0), 0)),
        out_shape=jax.ShapeDtypeStruct(x2d.shape, F32),
        scratch_shapes=act_scratch + act_scratch + [
            pltpu.VMEM((tm, D_MODEL), BF16),
            pltpu.VMEM((tm, D_MODEL), BF16),
            pltpu.VMEM((SUBLANES, 2 * M_QK), F32),
            pltpu.VMEM((M_HEADS, M_DQK, M_DV), F32),
            pltpu.VMEM((M_HEADS, 1, M_DQK), F32),
            pltpu.VMEM((M_HEADS, 1, GATE_LANES), F32),
            pltpu.VMEM((tm, M_V), BF16),
            pltpu.VMEM((tm, G_WIDTH), BF16),
            pltpu.VMEM((tm, M_QK), F32),
            pltpu.VMEM((tm, M_QK), F32),
            pltpu.VMEM((tm, GATE_LANES), F32),
            pltpu.VMEM((tm, M_V), F32),
            pltpu.VMEM((tm, G_WIDTH), F32),
            pltpu.VMEM((tm, G_WIDTH), BF16),
        ],
        compiler_params=pltpu.CompilerParams(
            dimension_semantics=("arbitrary",), vmem_limit_bytes=VMEM_LIMIT_BYTES),
        name="mixer",
    )(x2d, x2d, gains, *[w[name] for name in _MIX_WEIGHTS])


PACK_ROWS = 256


def _pack_w_in_kernel(w_ref, tail_ref, o_ref):
    gate_lo = 2 * M_QK + M_V
    n_gate = 2 * M_HEADS
    o_ref[:, :gate_lo] = w_ref[:, :gate_lo]
    lane = lax.broadcasted_iota(jnp.int32, (PACK_ROWS, GATE_LANES), 1)
    o_ref[:, gate_lo:gate_lo + GATE_LANES] = jnp.where(
        lane < n_gate, w_ref[:, gate_lo:gate_lo + GATE_LANES], jnp.zeros((), BF16))
    o_ref[:, gate_lo + GATE_LANES:] = jnp.concatenate(
        [w_ref[:, gate_lo:][:, n_gate:], tail_ref[...]], axis=1)


def _pack_w_in(w_in):
    depth, rows, cols = w_in.shape
    aligned = (cols // LANES) * LANES
    return pl.pallas_call(
        _pack_w_in_kernel,
        grid=(depth, rows // PACK_ROWS),
        in_specs=[pl.BlockSpec((None, PACK_ROWS, aligned), lambda l, i: (l, i, 0)),
                  pl.BlockSpec((None, PACK_ROWS, cols - aligned), lambda l, i: (l, i, 0))],
        out_specs=pl.BlockSpec((None, PACK_ROWS, _H_WIDTH), lambda l, i: (l, i, 0)),
        out_shape=jax.ShapeDtypeStruct((depth, rows, _H_WIDTH), BF16),
        compiler_params=pltpu.CompilerParams(
            dimension_semantics=("arbitrary", "arbitrary"), vmem_limit_bytes=VMEM_LIMIT_BYTES),
        name="pack_w_in",
    )(w_in[:, :, :aligned].astype(BF16), w_in[:, :, aligned:].astype(BF16))


def _pad_cols(w):
    return jnp.pad(w, ((0, 0), (0, 0), (0, PAD_COLS))).astype(BF16)


def kernel(x, norm_gains, ffn1_w_in, ffn1_w_out, w_in, conv_w, conv_b, i_bias, f_bias, m_norm_g, g_norm_g, g_norm_b, w_s, b_s, p_a, p_b, w_out, ffn2_w_in, ffn2_w_out):
    b, s, d = x.shape
    w = dict(
        h=_pack_w_in(w_in),
        p=_pad_cols(jnp.concatenate([p_a, p_b, w_out], axis=-1)),
        conv_w=conv_w,
        conv_b=conv_b[:, None, :],
        if_bias=jnp.pad(jnp.concatenate([i_bias, f_bias], axis=-1),
                        ((0, 0), (0, GATE_LANES - 2 * M_HEADS)))[:, None, :],
        m_norm_g=m_norm_g[:, None, :],
        g_norm_g=g_norm_g[:, None, :],
        g_norm_b=g_norm_b[:, None, :],
        w_s=w_s,
        b_s_t=jnp.swapaxes(b_s, 1, 2),
    )
    ffn_w = [(ffn1_w_in, ffn1_w_out), (ffn2_w_in, ffn2_w_out)]
    w_in_bf, w_out_bf = ffn1_w_in[0].astype(BF16), ffn1_w_out[0].astype(BF16)

    x2d = x.reshape(b * s, d)
    for layer in range(DEPTH):
        for which, (gi, go) in enumerate(((0, 1), (4, 5))):
            call = 2 * layer + which
            nxt = None
            if call + 1 < 2 * DEPTH:
                nxt = ffn_w[(call + 1) % 2] + ((call + 1) // 2,)
            x2d, w_in_bf, w_out_bf = _ffn(x2d, norm_gains, w_in_bf, w_out_bf, layer, gi, go, nxt)
            if which == 0:
                x2d = _mixer(x2d, norm_gains, w, layer, s // MIX_TM)
    return x2d.reshape(b, s, d)
```

```python
import functools

import jax
import jax.numpy as jnp
from jax import lax
from jax.experimental import pallas as pl
from jax.experimental.pallas import tpu as pltpu

D_MODEL = 1024
DEPTH = 4
M_HEADS = 4
M_DQK = 128
M_DV = 256
M_QK = M_HEADS * M_DQK
M_V = M_HEADS * M_DV
CHUNK = 128
CONV_W = 4
G_GROUPS = 4
G_WIDTH = 1024
G_DG = G_WIDTH // G_GROUPS
D_FF = 2816
EPS = 1e-6
GATE_LANES = 128
SUBLANES = 8
MXU_COLS = 256

FFN_TM = 512
FFN_ROWS = 256
FFN_CAST_ROWS = 128
MIX_TM = 256
VMEM_LIMIT_BYTES = 56 * 1024 * 1024

BF16 = jnp.bfloat16
F32 = jnp.float32


def _dot(a, b):
    return jnp.dot(a, b, preferred_element_type=F32)


def _rms(x, g):
    return x * lax.rsqrt(jnp.mean(x * x, axis=-1, keepdims=True) + EPS) * g


def _ln(x, g):
    mu = jnp.mean(x, axis=-1, keepdims=True)
    xc = x - mu
    var = jnp.mean(xc * xc, axis=-1, keepdims=True)
    return xc * lax.rsqrt(var + EPS) * g


def _log_sigmoid(x):
    return jnp.minimum(x, 0.0) - jnp.log1p(jnp.exp(-jnp.abs(x)))


def _cast_stream(t, src_hbm, dst_hbm, stage_f32, stage_bf16, sem_in, sem_out, first_step):
    chunk = stage_f32.shape[1]
    n = src_hbm.shape[0] // chunk
    k = t - first_step

    def rows(kk):
        return pl.ds(pl.multiple_of(kk * chunk, chunk), chunk)

    def fetch(kk):
        return pltpu.make_async_copy(src_hbm.at[rows(kk), :], stage_f32.at[kk % 2],
                                     sem_in.at[kk % 2])

    def write_back(kk):
        return pltpu.make_async_copy(stage_bf16.at[kk % 2], dst_hbm.at[rows(kk), :],
                                     sem_out.at[kk % 2])

    @pl.when(k == 0)
    def _():
        fetch(k).start()

    @pl.when(jnp.logical_and(k >= 0, k < n))
    def _():
        @pl.when(k + 1 < n)
        def _():
            fetch(k + 1).start()

        fetch(k).wait()

        @pl.when(k >= 2)
        def _():
            write_back(k - 2).wait()

        stage_bf16[k % 2] = stage_f32[k % 2].astype(BF16)
        write_back(k).start()

    @pl.when(k == n - 1)
    def _():
        write_back(k - 1).wait()
        write_back(k).wait()


def _ffn_body(x_ref, gains_ref, w_in_ref, w_out_ref, o_ref, gi, go):
    for r in range(0, FFN_TM, FFN_ROWS):
        rows = slice(r, r + FFN_ROWS)
        x = x_ref[rows, :]
        h = _rms(x, gains_ref[gi:gi + 1, :]).astype(BF16)
        a = _dot(h, w_in_ref[:, :D_FF])
        g = _dot(h, w_in_ref[:, D_FF:])
        s = (jax.nn.silu(a) * g).astype(BF16)
        y = _dot(s, w_out_ref[...])
        o_ref[rows, :] = x + 0.5 * _rms(y, gains_ref[go:go + 1, :])


def _ffn_kernel(x_ref, gains_ref, w_in_ref, w_out_ref, o_ref, *, gi, go):
    _ffn_body(x_ref, gains_ref, w_in_ref, w_out_ref, o_ref, gi, go)


def _ffn_prep_kernel(x_ref, gains_ref, w_in_ref, w_out_ref, nxt_in_hbm, nxt_out_hbm,
                     o_ref, cast_in_hbm, cast_out_hbm,
                     in_f32, in_bf16, out_f32, out_bf16, sem_a, sem_b, sem_c, sem_d,
                     *, gi, go, nxt_layer):
    t = pl.program_id(0)
    n_in = D_MODEL // FFN_CAST_ROWS
    _cast_stream(t, nxt_in_hbm.at[nxt_layer], cast_in_hbm, in_f32, in_bf16,
                 sem_a, sem_b, 0)
    _cast_stream(t, nxt_out_hbm.at[nxt_layer], cast_out_hbm, out_f32, out_bf16,
                 sem_c, sem_d, n_in)
    _ffn_body(x_ref, gains_ref, w_in_ref, w_out_ref, o_ref, gi, go)


def _resident(shape, index_map):
    return pl.BlockSpec(shape, index_map, pipeline_mode=pl.Buffered(1))


def _ffn(x2d, gains, w_in, w_out, layer, gi, go, nxt=None):
    n_steps = x2d.shape[0] // FFN_TM
    in_specs = [
        pl.BlockSpec((FFN_TM, D_MODEL), lambda i: (i, 0)),
        _resident((None, 6, D_MODEL), lambda i: (layer, 0, 0)),
        _resident((D_MODEL, 2 * D_FF), lambda i: (0, 0)),
        _resident((D_FF, D_MODEL), lambda i: (0, 0)),
    ]
    x_spec = pl.BlockSpec((FFN_TM, D_MODEL), lambda i: (i, 0))
    x_shape = jax.ShapeDtypeStruct(x2d.shape, F32)
    params = pltpu.CompilerParams(
        dimension_semantics=("arbitrary",), vmem_limit_bytes=VMEM_LIMIT_BYTES)
    if nxt is None:
        out = pl.pallas_call(
            functools.partial(_ffn_kernel, gi=gi, go=go),
            grid=(n_steps,), in_specs=in_specs, out_specs=x_spec, out_shape=x_shape,
            compiler_params=params, name="ffn",
        )(x2d, gains, w_in, w_out)
        return out, None, None
    nxt_in, nxt_out, nxt_layer = nxt
    assert (D_MODEL + D_FF) // FFN_CAST_ROWS <= n_steps
    hbm = pl.BlockSpec(memory_space=pl.ANY)
    return pl.pallas_call(
        functools.partial(_ffn_prep_kernel, gi=gi, go=go, nxt_layer=nxt_layer),
        grid=(n_steps,),
        in_specs=in_specs + [hbm, hbm],
        out_specs=[x_spec, hbm, hbm],
        out_shape=[x_shape, jax.ShapeDtypeStruct((D_MODEL, 2 * D_FF), BF16),
                   jax.ShapeDtypeStruct((D_FF, D_MODEL), BF16)],
        scratch_shapes=[
            pltpu.VMEM((2, FFN_CAST_ROWS, 2 * D_FF), F32),
            pltpu.VMEM((2, FFN_CAST_ROWS, 2 * D_FF), BF16),
            pltpu.VMEM((2, FFN_CAST_ROWS, D_MODEL), F32),
            pltpu.VMEM((2, FFN_CAST_ROWS, D_MODEL), BF16),
            pltpu.SemaphoreType.DMA((2,)), pltpu.SemaphoreType.DMA((2,)),
            pltpu.SemaphoreType.DMA((2,)), pltpu.SemaphoreType.DMA((2,)),
        ],
        compiler_params=params, name="ffn_prep",
    )(x2d, gains, w_in, w_out, nxt_in, nxt_out)


_ACT_SHAPES = (
    ("qk", 2 * M_QK, F32), ("if", GATE_LANES, F32), ("v", M_V, BF16), ("o", M_V, F32),
    ("u", G_WIDTH, F32), ("vg", G_WIDTH, F32), ("ga", D_MODEL, F32), ("gb", D_MODEL, F32),
)
_N_ACT = len(_ACT_SHAPES)
_MIX_WEIGHTS = ("h", "p", "conv_w", "conv_b", "if_bias",
                "m_norm_g", "g_norm_g", "g_norm_b", "w_s", "b_s_t")

LANES = 128
PAD_COLS = LANES


def _col_ranges(names_widths):
    out, off = {}, 0
    for name, width in names_widths:
        out[name] = (off, off + width)
        off += width
    return out, off


_H_COLS, _H_WIDTH = _col_ranges((("qk", 2 * M_QK), ("v", M_V), ("if", GATE_LANES), ("o", M_V),
                                 ("u", G_WIDTH), ("vg", G_WIDTH), ("ga", D_MODEL), ("gb", D_MODEL)))
_P_COLS, _P_WIDTH = _col_ranges((("p_a", D_MODEL), ("p_b", D_MODEL), ("w_out", D_MODEL),
                                 ("pad", PAD_COLS)))


def _chunk_cumsum(v, row_in_chunk):
    shift = 1
    while shift < CHUNK:
        rolled = pltpu.roll(v, shift, axis=0)
        v = v + jnp.where(row_in_chunk >= shift, rolled, 0.0)
        shift *= 2
    return v


def _shift_rows(cur, tail, s, row_in_vreg):
    rolled = pltpu.roll(cur, s, axis=0)
    head = jnp.where(row_in_vreg < s, pltpu.roll(tail, s, axis=0), rolled[0:SUBLANES])
    return jnp.concatenate([head, rolled[SUBLANES:]], axis=0)


def _mixer_step(xn_ref, xb_ref, gains_ref, w, o_ref, act_w, act_r, h_cur_ref, h_next_ref,
                tail_ref, ct_ref, n_ref, m_ref, ya_ref, hg_ref, q_ref, k_ref, gates_ref,
                og_ref, u_ref, vln_ref):
    tm = MIX_TM
    nch = tm // CHUNK

    tri_r = lax.broadcasted_iota(jnp.int32, (CHUNK, CHUNK), 0)
    tri_c = lax.broadcasted_iota(jnp.int32, (CHUNK, CHUNK), 1)
    causal = tri_r >= tri_c

    def mlstm_unit(c, hd):
        rows = slice(c * CHUNK, (c + 1) * CHUNK)
        qk_cols = slice(hd * M_DQK, (hd + 1) * M_DQK)
        v_cols = slice(hd * M_DV, (hd + 1) * M_DV)
        gc = gates_ref[rows, :]
        gct = gc.T
        q = q_ref[rows, qk_cols]
        k = k_ref[rows, qk_cols]
        v = act_r["v"][rows, v_cols]
        i_col = gc[:, hd:hd + 1]
        b_col = gc[:, M_HEADS + hd:M_HEADS + hd + 1]
        i_row = gct[hd:hd + 1, :]
        b_row = gct[M_HEADS + hd:M_HEADS + hd + 1, :]
        b_last = b_row[:, CHUNK - 1:CHUNK]
        m_prev = m_ref[hd][:, 0:1]
        ct = ct_ref[hd]
        n_row = n_ref[hd]

        a_row = i_row - b_row
        d = jnp.where(causal, b_col + a_row, -jnp.inf)
        inter = b_col + m_prev
        m_t = jnp.maximum(inter, jnp.max(d, axis=-1, keepdims=True))
        s = lax.dot_general(q.astype(BF16), k.astype(BF16), (((1,), (1,)), ((), ())),
                            preferred_element_type=F32)
        p = jnp.exp(d - m_t) * s
        s_inter = jnp.exp(inter - m_t)
        lhs = jnp.concatenate([p, s_inter * q], axis=1).astype(BF16)
        rhs = jnp.concatenate([v, ct.astype(BF16)], axis=0)
        num = _dot(lhs, rhs)
        den = (jnp.sum(p, axis=-1, keepdims=True)
               + s_inter * jnp.sum(q * n_row, axis=-1, keepdims=True))
        hh = num / jnp.maximum(jnp.abs(den), jnp.exp(-m_t))

        g_row = b_last + a_row
        m_new = jnp.maximum(b_last + m_prev, jnp.max(g_row, axis=-1, keepdims=True))
        w_col = jnp.exp(b_last - b_col + i_col - m_new)
        decay = jnp.exp(b_last + m_prev - m_new)
        kw = w_col * k
        ct_ref[hd] = decay * ct + lax.dot_general(
            kw.astype(BF16), v, (((0,), (0,)), ((), ())), preferred_element_type=F32)
        n_ref[hd] = decay * n_row + jnp.sum(kw, axis=0, keepdims=True)
        m_ref[hd] = jnp.broadcast_to(m_new, (1, GATE_LANES))

        hm = _ln(hh, w["m_norm_g"][:, v_cols])
        ya_ref[rows, v_cols] = (og_ref[rows, v_cols] * hm).astype(BF16)

    def gmlp_group(grp):
        w_tri = jnp.where(causal, w["w_s"][grp], 0.0).astype(BF16)
        bias_col = w["b_s_t"][:, grp:grp + 1]
        cols = slice(grp * G_DG, (grp + 1) * G_DG)
        u_ref[:, cols] = jax.nn.gelu(act_r["u"][:, cols])
        for c in range(nch):
            rows = slice(c * CHUNK, (c + 1) * CHUNK)
            sg = _dot(w_tri, vln_ref[rows, cols]) + bias_col
            hg_ref[rows, cols] = (u_ref[rows, cols] * sg).astype(BF16)

    def wp(name):
        lo, hi = _P_COLS[name]
        return w["p"][:, lo:hi]

    def proj_block(name, lo, hi):
        base = _H_COLS[name][0]
        z = _dot(h_cur_ref[...], w["h"][:, base + lo:base + hi])
        act_w[name][:, lo:hi] = z.astype(act_w[name].dtype)

    def gates():
        pre = act_r["if"][...] + w["if_bias"][...]
        col = lax.broadcasted_iota(jnp.int32, (tm, GATE_LANES), 1)
        row_in_chunk = lax.broadcasted_iota(jnp.int32, (tm, GATE_LANES), 0) % CHUNK
        gates_ref[...] = jnp.where(
            col >= M_HEADS, _chunk_cumsum(_log_sigmoid(pre), row_in_chunk), pre)

    def conv_block(lo, hi):
        zqk = act_r["qk"][:, lo:hi]
        row_in_vreg = lax.broadcasted_iota(jnp.int32, (SUBLANES, hi - lo), 0)
        tail = tail_ref[:, lo:hi]
        conv = w["conv_b"][:, lo:hi] + w["conv_w"][CONV_W - 1:CONV_W, lo:hi] * zqk
        for s in range(1, CONV_W):
            j = CONV_W - 1 - s
            conv = conv + w["conv_w"][j:j + 1, lo:hi] * _shift_rows(zqk, tail, s, row_in_vreg)
        tail_ref[:, lo:hi] = zqk[tm - SUBLANES:tm, :]
        y = jax.nn.silu(conv)
        if lo < M_QK:
            q_ref[:, lo:hi] = y * (M_DQK ** -0.5)
        else:
            k_ref[:, lo - M_QK:hi - M_QK] = y

    def out_gate(hd):
        v_cols = slice(hd * M_DV, (hd + 1) * M_DV)
        og_ref[:, v_cols] = jax.nn.sigmoid(act_r["o"][:, v_cols])

    def gmlp_norm():
        vln_ref[...] = (_ln(jax.nn.gelu(act_r["vg"][...]), w["g_norm_g"][...])
                        + w["g_norm_b"][...]).astype(BF16)

    branch_b = {}

    def gmlp_out():
        branch_b["y"] = _dot(hg_ref[...], wp("p_b"))

    proj = []
    for name, width, _ in _ACT_SHAPES:
        for lo in range(0, width, MXU_COLS):
            hi = min(lo + MXU_COLS, width)
            proj.append((hi - lo, functools.partial(proj_block, name, lo, hi)))
    rec = [(300, gates)]
    rec += [(350, functools.partial(conv_block, lo, lo + MXU_COLS))
            for lo in range(0, 2 * M_QK, MXU_COLS)]
    rec += [(100, functools.partial(out_gate, hd)) for hd in range(M_HEADS)]
    rec += [(450, functools.partial(mlstm_unit, 0, hd)) for hd in range(M_HEADS)]
    rec += [(900, gmlp_norm)]
    late = [(450, functools.partial(mlstm_unit, c, hd))
            for c in range(1, nch) for hd in range(M_HEADS)]
    rec += late[:len(late) // 2]
    rec += [(300, functools.partial(gmlp_group, grp)) for grp in range(G_GROUPS)]
    rec += [(0, gmlp_out)]
    rec += late[len(late) // 2:]
    proj_total = sum(c for c, _ in proj)
    rec_total = sum(c for c, _ in rec)
    done_proj, done_rec, ip = 0, 0, 0
    for cost, run in rec:
        while ip < len(proj) and done_proj * rec_total <= done_rec * proj_total:
            done_proj += proj[ip][0]
            proj[ip][1]()
            ip += 1
        run()
        done_rec += cost
    for _, run in proj[ip:]:
        run()

    h_next_ref[...] = _rms(xn_ref[...], gains_ref[2:3, :]).astype(BF16)
    y_a = _dot(ya_ref[...], wp("p_a"))
    merged = (jax.nn.sigmoid(act_r["ga"][...]) * y_a
              + jax.nn.sigmoid(act_r["gb"][...]) * branch_b["y"])
    mix = _dot(merged.astype(BF16), wp("w_out"))
    o_ref[...] = xb_ref[...] + _rms(mix, gains_ref[3:4, :])


def _mixer_kernel(*refs, tiles_per_row):
    xn_ref, xb_ref, gains_ref = refs[:3]
    nw = len(_MIX_WEIGHTS)
    w = dict(zip(_MIX_WEIGHTS, refs[3:3 + nw]))
    o_ref = refs[3 + nw]
    scratch = refs[4 + nw:]
    act = [dict(zip((n for n, _, _ in _ACT_SHAPES), scratch[i * _N_ACT:(i + 1) * _N_ACT]))
           for i in range(2)]
    h_refs = scratch[2 * _N_ACT:2 * _N_ACT + 2]
    rest = scratch[2 * _N_ACT + 2:]
    tail_ref, ct_ref, n_ref, m_ref = rest[:4]
    t = pl.program_id(0)

    @pl.when(t == 0)
    def _():
        for ref in act[1].values():
            ref[...] = jnp.zeros_like(ref)
        h_refs[0][...] = _rms(xb_ref[...], gains_ref[2:3, :]).astype(BF16)

    @pl.when(jnp.logical_or(t == 0, (t + tiles_per_row - 1) % tiles_per_row == 0))
    def _():
        tail_ref[...] = jnp.zeros_like(tail_ref)
        ct_ref[...] = jnp.zeros_like(ct_ref)
        n_ref[...] = jnp.zeros_like(n_ref)
        m_ref[...] = jnp.zeros_like(m_ref)

    step = functools.partial(_mixer_step, xn_ref, xb_ref, gains_ref, w, o_ref)

    @pl.when(t % 2 == 0)
    def _():
        step(act[0], act[1], h_refs[0], h_refs[1], *rest)

    @pl.when(t % 2 == 1)
    def _():
        step(act[1], act[0], h_refs[1], h_refs[0], *rest)


def _mixer(x2d, gains, w, layer, tiles_per_row):
    tm = MIX_TM
    n_tiles = x2d.shape[0] // tm

    def lay(shape):
        nd = len(shape)
        return _resident((None,) + shape, lambda t: (layer,) + (0,) * nd)

    w_specs = {
        "h": lay((D_MODEL, _H_WIDTH)), "p": lay((D_MODEL, _P_WIDTH)),
        "conv_w": lay((CONV_W, 2 * M_QK)), "conv_b": lay((1, 2 * M_QK)),
        "if_bias": lay((1, GATE_LANES)), "m_norm_g": lay((1, M_V)),
        "g_norm_g": lay((1, G_WIDTH)), "g_norm_b": lay((1, G_WIDTH)),
        "w_s": lay((G_GROUPS, CHUNK, CHUNK)), "b_s_t": lay((CHUNK, G_GROUPS)),
    }
    in_specs = [
        pl.BlockSpec((tm, D_MODEL), lambda t: (jnp.minimum(t + 1, n_tiles - 1), 0)),
        pl.BlockSpec((tm, D_MODEL), lambda t: (jnp.maximum(t - 1, 0), 0)),
        lay((6, D_MODEL)),
    ] + [w_specs[name] for name in _MIX_WEIGHTS]
    act_scratch = [pltpu.VMEM((tm, width), dt) for _, width, dt in _ACT_SHAPES]
    return pl.pallas_call(
        functools.partial(_mixer_kernel, tiles_per_row=tiles_per_row),
        grid=(n_tiles + 1,),
        in_specs=in_specs,
        out_specs=pl.BlockSpec((tm, D_MODEL), lambda t: (jnp.maximum(t - 1, 0), 0)),
        out_shape=jax.ShapeDtypeStruct(x2d.shape, F32),
        scratch_shapes=act_scratch + act_scratch + [
            pltpu.VMEM((tm, D_MODEL), BF16),
            pltpu.VMEM((tm, D_MODEL), BF16),
            pltpu.VMEM((SUBLANES, 2 * M_QK), F32),
            pltpu.VMEM((M_HEADS, M_DQK, M_DV), F32),
            pltpu.VMEM((M_HEADS, 1, M_DQK), F32),
            pltpu.VMEM((M_HEADS, 1, GATE_LANES), F32),
            pltpu.VMEM((tm, M_V), BF16),
            pltpu.VMEM((tm, G_WIDTH), BF16),
            pltpu.VMEM((tm, M_QK), F32),
            pltpu.VMEM((tm, M_QK), F32),
            pltpu.VMEM((tm, GATE_LANES), F32),
            pltpu.VMEM((tm, M_V), F32),
            pltpu.VMEM((tm, G_WIDTH), F32),
            pltpu.VMEM((tm, G_WIDTH), BF16),
        ],
        compiler_params=pltpu.CompilerParams(
            dimension_semantics=("arbitrary",), vmem_limit_bytes=VMEM_LIMIT_BYTES),
        name="mixer",
    )(x2d, x2d, gains, *[w[name] for name in _MIX_WEIGHTS])


PACK_ROWS = 256


def _pack_w_in_kernel(w_ref, o_ref):
    gate_lo = 2 * M_QK + M_V
    n_gate = 2 * M_HEADS
    o_ref[:, :gate_lo] = w_ref[:, :gate_lo]
    lane = lax.broadcasted_iota(jnp.int32, (PACK_ROWS, GATE_LANES), 1)
    o_ref[:, gate_lo:gate_lo + GATE_LANES] = jnp.where(
        lane < n_gate, w_ref[:, gate_lo:gate_lo + GATE_LANES], jnp.zeros((), BF16))
    rest = _H_WIDTH - gate_lo - GATE_LANES
    o_ref[:, gate_lo + GATE_LANES:] = w_ref[:, gate_lo:][:, n_gate:n_gate + rest]


def _pack_w_in(w_in):
    depth, rows, cols = w_in.shape
    padded = jnp.pad(w_in, ((0, 0), (0, 0), (0, _H_WIDTH - cols))).astype(BF16)
    return pl.pallas_call(
        _pack_w_in_kernel,
        grid=(depth, rows // PACK_ROWS),
        in_specs=[pl.BlockSpec((None, PACK_ROWS, _H_WIDTH), lambda l, i: (l, i, 0))],
        out_specs=pl.BlockSpec((None, PACK_ROWS, _H_WIDTH), lambda l, i: (l, i, 0)),
        out_shape=jax.ShapeDtypeStruct((depth, rows, _H_WIDTH), BF16),
        compiler_params=pltpu.CompilerParams(
            dimension_semantics=("arbitrary", "arbitrary"), vmem_limit_bytes=VMEM_LIMIT_BYTES),
        name="pack_w_in",
    )(padded)


def _pad_cols(w):
    return jnp.pad(w, ((0, 0), (0, 0), (0, PAD_COLS))).astype(BF16)


def kernel(x, norm_gains, ffn1_w_in, ffn1_w_out, w_in, conv_w, conv_b, i_bias, f_bias, m_norm_g, g_norm_g, g_norm_b, w_s, b_s, p_a, p_b, w_out, ffn2_w_in, ffn2_w_out):
    b, s, d = x.shape
    w = dict(
        h=_pack_w_in(w_in),
        p=_pad_cols(jnp.concatenate([p_a, p_b, w_out], axis=-1)),
        conv_w=conv_w,
        conv_b=conv_b[:, None, :],
        if_bias=jnp.pad(jnp.concatenate([i_bias, f_bias], axis=-1),
                        ((0, 0), (0, GATE_LANES - 2 * M_HEADS)))[:, None, :],
        m_norm_g=m_norm_g[:, None, :],
        g_norm_g=g_norm_g[:, None, :],
        g_norm_b=g_norm_b[:, None, :],
        w_s=w_s,
        b_s_t=jnp.swapaxes(b_s, 1, 2),
    )
    ffn_w = [(ffn1_w_in, ffn1_w_out), (ffn2_w_in, ffn2_w_out)]
    w_in_bf, w_out_bf = ffn1_w_in[0].astype(BF16), ffn1_w_out[0].astype(BF16)

    x2d = x.reshape(b * s, d)
    for layer in range(DEPTH):
        for which, (gi, go) in enumerate(((0, 1), (4, 5))):
            call = 2 * layer + which
            nxt = None
            if call + 1 < 2 * DEPTH:
                nxt = ffn_w[(call + 1) % 2] + ((call + 1) // 2,)
            x2d, w_in_bf, w_out_bf = _ffn(x2d, norm_gains, w_in_bf, w_out_bf, layer, gi, go, nxt)
            if which == 0:
                x2d = _mixer(x2d, norm_gains, w, layer, s // MIX_TM)
    return x2d.reshape(b, s, d)
```

```python
import functools

import jax
import jax.numpy as jnp
from jax import lax
from jax.experimental import pallas as pl
from jax.experimental.pallas import tpu as pltpu

D_MODEL = 1024
DEPTH = 4
M_HEADS = 4
M_DQK = 128
M_DV = 256
M_QK = M_HEADS * M_DQK
M_V = M_HEADS * M_DV
CHUNK = 128
CONV_W = 4
G_GROUPS = 4
G_WIDTH = 1024
G_DG = G_WIDTH // G_GROUPS
D_FF = 2816
EPS = 1e-6
GATE_LANES = 128
SUBLANES = 8
MXU_COLS = 256

FFN_TM = 512
FFN_ROWS = 256
FFN_CAST_ROWS = 128
MIX_TM = 256
VMEM_LIMIT_BYTES = 56 * 1024 * 1024

BF16 = jnp.bfloat16
F32 = jnp.float32


def _dot(a, b):
    return jnp.dot(a, b, preferred_element_type=F32)


def _rms(x, g):
    return x * lax.rsqrt(jnp.mean(x * x, axis=-1, keepdims=True) + EPS) * g


def _ln(x, g):
    mu = jnp.mean(x, axis=-1, keepdims=True)
    xc = x - mu
    var = jnp.mean(xc * xc, axis=-1, keepdims=True)
    return xc * lax.rsqrt(var + EPS) * g


def _log_sigmoid(x):
    return jnp.minimum(x, 0.0) - jnp.log1p(jnp.exp(-jnp.abs(x)))


def _cast_stream(t, src_hbm, dst_hbm, stage_f32, stage_bf16, sem_in, sem_out, first_step):
    chunk = stage_f32.shape[1]
    n = src_hbm.shape[0] // chunk
    k = t - first_step

    def rows(kk):
        return pl.ds(pl.multiple_of(kk * chunk, chunk), chunk)

    def fetch(kk):
        return pltpu.make_async_copy(src_hbm.at[rows(kk), :], stage_f32.at[kk % 2],
                                     sem_in.at[kk % 2])

    def write_back(kk):
        return pltpu.make_async_copy(stage_bf16.at[kk % 2], dst_hbm.at[rows(kk), :],
                                     sem_out.at[kk % 2])

    @pl.when(k == 0)
    def _():
        fetch(k).start()

    @pl.when(jnp.logical_and(k >= 0, k < n))
    def _():
        @pl.when(k + 1 < n)
        def _():
            fetch(k + 1).start()

        fetch(k).wait()

        @pl.when(k >= 2)
        def _():
            write_back(k - 2).wait()

        stage_bf16[k % 2] = stage_f32[k % 2].astype(BF16)
        write_back(k).start()

    @pl.when(k == n - 1)
    def _():
        write_back(k - 1).wait()
        write_back(k).wait()


def _ffn_body(x_ref, gains_ref, w_in_ref, w_out_ref, o_ref, gi, go):
    for r in range(0, FFN_TM, FFN_ROWS):
        rows = slice(r, r + FFN_ROWS)
        x = x_ref[rows, :]
        h = _rms(x, gains_ref[gi:gi + 1, :]).astype(BF16)
        a = _dot(h, w_in_ref[:, :D_FF])
        g = _dot(h, w_in_ref[:, D_FF:])
        s = (jax.nn.silu(a) * g).astype(BF16)
        y = _dot(s, w_out_ref[...])
        o_ref[rows, :] = x + 0.5 * _rms(y, gains_ref[go:go + 1, :])


def _ffn_kernel(x_ref, gains_ref, w_in_ref, w_out_ref, o_ref, *, gi, go):
    _ffn_body(x_ref, gains_ref, w_in_ref, w_out_ref, o_ref, gi, go)


def _ffn_prep_kernel(x_ref, gains_ref, w_in_ref, w_out_ref, nxt_in_hbm, nxt_out_hbm,
                     o_ref, cast_in_hbm, cast_out_hbm,
                     in_f32, in_bf16, out_f32, out_bf16, sem_a, sem_b, sem_c, sem_d,
                     *, gi, go, nxt_layer):
    _ffn_body(x_ref, gains_ref, w_in_ref, w_out_ref, o_ref, gi, go)
    t = pl.program_id(0)
    n_in = D_MODEL // FFN_CAST_ROWS
    _cast_stream(t, nxt_in_hbm.at[nxt_layer], cast_in_hbm, in_f32, in_bf16,
                 sem_a, sem_b, 0)
    _cast_stream(t, nxt_out_hbm.at[nxt_layer], cast_out_hbm, out_f32, out_bf16,
                 sem_c, sem_d, n_in)


def _resident(shape, index_map):
    return pl.BlockSpec(shape, index_map, pipeline_mode=pl.Buffered(1))


def _ffn(x2d, gains, w_in, w_out, layer, gi, go, nxt=None):
    n_steps = x2d.shape[0] // FFN_TM
    in_specs = [
        pl.BlockSpec((FFN_TM, D_MODEL), lambda i: (i, 0)),
        _resident((None, 6, D_MODEL), lambda i: (layer, 0, 0)),
        _resident((D_MODEL, 2 * D_FF), lambda i: (0, 0)),
        _resident((D_FF, D_MODEL), lambda i: (0, 0)),
    ]
    x_spec = pl.BlockSpec((FFN_TM, D_MODEL), lambda i: (i, 0))
    x_shape = jax.ShapeDtypeStruct(x2d.shape, F32)
    params = pltpu.CompilerParams(
        dimension_semantics=("arbitrary",), vmem_limit_bytes=VMEM_LIMIT_BYTES)
    if nxt is None:
        out = pl.pallas_call(
            functools.partial(_ffn_kernel, gi=gi, go=go),
            grid=(n_steps,), in_specs=in_specs, out_specs=x_spec, out_shape=x_shape,
            compiler_params=params, name="ffn",
        )(x2d, gains, w_in, w_out)
        return out, None, None
    nxt_in, nxt_out, nxt_layer = nxt
    assert (D_MODEL + D_FF) // FFN_CAST_ROWS <= n_steps
    hbm = pl.BlockSpec(memory_space=pl.ANY)
    return pl.pallas_call(
        functools.partial(_ffn_prep_kernel, gi=gi, go=go, nxt_layer=nxt_layer),
        grid=(n_steps,),
        in_specs=in_specs + [hbm, hbm],
        out_specs=[x_spec, hbm, hbm],
        out_shape=[x_shape, jax.ShapeDtypeStruct((D_MODEL, 2 * D_FF), BF16),
                   jax.ShapeDtypeStruct((D_FF, D_MODEL), BF16)],
        scratch_shapes=[
            pltpu.VMEM((2, FFN_CAST_ROWS, 2 * D_FF), F32),
            pltpu.VMEM((2, FFN_CAST_ROWS, 2 * D_FF), BF16),
            pltpu.VMEM((2, FFN_CAST_ROWS, D_MODEL), F32),
            pltpu.VMEM((2, FFN_CAST_ROWS, D_MODEL), BF16),
            pltpu.SemaphoreType.DMA((2,)), pltpu.SemaphoreType.DMA((2,)),
            pltpu.SemaphoreType.DMA((2,)), pltpu.SemaphoreType.DMA((2,)),
        ],
        compiler_params=params, name="ffn_prep",
    )(x2d, gains, w_in, w_out, nxt_in, nxt_out)


_ACT_SHAPES = (
    ("qk", 2 * M_QK, F32), ("if", GATE_LANES, F32), ("v", M_V, BF16), ("o", M_V, F32),
    ("u", G_WIDTH, F32), ("vg", G_WIDTH, F32), ("ga", D_MODEL, F32), ("gb", D_MODEL, F32),
)
_N_ACT = len(_ACT_SHAPES)
_MIX_WEIGHTS = ("h", "p", "conv_w", "conv_b", "if_bias",
                "m_norm_g", "g_norm_g", "g_norm_b", "w_s", "b_s_t")

LANES = 128
PAD_COLS = LANES


def _col_ranges(names_widths):
    out, off = {}, 0
    for name, width in names_widths:
        out[name] = (off, off + width)
        off += width
    return out, off


_H_COLS, _H_WIDTH = _col_ranges((("qk", 2 * M_QK), ("v", M_V), ("if", GATE_LANES), ("o", M_V),
                                 ("u", G_WIDTH), ("vg", G_WIDTH), ("ga", D_MODEL), ("gb", D_MODEL)))
_P_COLS, _P_WIDTH = _col_ranges((("p_a", D_MODEL), ("p_b", D_MODEL), ("w_out", D_MODEL),
                                 ("pad", PAD_COLS)))


def _chunk_cumsum(v, row_in_chunk):
    shift = 1
    while shift < CHUNK:
        rolled = pltpu.roll(v, shift, axis=0)
        v = v + jnp.where(row_in_chunk >= shift, rolled, 0.0)
        shift *= 2
    return v


def _shift_rows(cur, tail, s, row_in_vreg):
    rolled = pltpu.roll(cur, s, axis=0)
    head = jnp.where(row_in_vreg < s, pltpu.roll(tail, s, axis=0), rolled[0:SUBLANES])
    return jnp.concatenate([head, rolled[SUBLANES:]], axis=0)


def _mixer_step(xn_ref, xb_ref, gains_ref, w, o_ref, act_w, act_r, h_cur_ref, h_next_ref,
                tail_ref, ct_ref, n_ref, m_ref, ya_ref, hg_ref, q_ref, k_ref, gates_ref,
                og_ref, u_ref, vln_ref):
    tm = MIX_TM
    nch = tm // CHUNK

    tri_r = lax.broadcasted_iota(jnp.int32, (CHUNK, CHUNK), 0)
    tri_c = lax.broadcasted_iota(jnp.int32, (CHUNK, CHUNK), 1)
    causal = tri_r >= tri_c

    def mlstm_unit(c, hd):
        rows = slice(c * CHUNK, (c + 1) * CHUNK)
        qk_cols = slice(hd * M_DQK, (hd + 1) * M_DQK)
        v_cols = slice(hd * M_DV, (hd + 1) * M_DV)
        gc = gates_ref[rows, :]
        gct = gc.T
        q = q_ref[rows, qk_cols]
        k = k_ref[rows, qk_cols]
        v = act_r["v"][rows, v_cols]
        i_col = gc[:, hd:hd + 1]
        b_col = gc[:, M_HEADS + hd:M_HEADS + hd + 1]
        i_row = gct[hd:hd + 1, :]
        b_row = gct[M_HEADS + hd:M_HEADS + hd + 1, :]
        b_last = b_row[:, CHUNK - 1:CHUNK]
        m_prev = m_ref[hd][:, 0:1]
        ct = ct_ref[hd]
        n_row = n_ref[hd]

        a_row = i_row - b_row
        d = jnp.where(causal, b_col + a_row, -jnp.inf)
        inter = b_col + m_prev
        m_t = jnp.maximum(inter, jnp.max(d, axis=-1, keepdims=True))
        s = lax.dot_general(q.astype(BF16), k.astype(BF16), (((1,), (1,)), ((), ())),
                            preferred_element_type=F32)
        p = jnp.exp(d - m_t) * s
        s_inter = jnp.exp(inter - m_t)
        lhs = jnp.concatenate([p, s_inter * q], axis=1).astype(BF16)
        rhs = jnp.concatenate([v, ct.astype(BF16)], axis=0)
        num = _dot(lhs, rhs)
        den = (jnp.sum(p, axis=-1, keepdims=True)
               + s_inter * jnp.sum(q * n_row, axis=-1, keepdims=True))
        hh = num / jnp.maximum(jnp.abs(den), jnp.exp(-m_t))

        g_row = b_last + a_row
        m_new = jnp.maximum(b_last + m_prev, jnp.max(g_row, axis=-1, keepdims=True))
        w_col = jnp.exp(b_last - b_col + i_col - m_new)
        decay = jnp.exp(b_last + m_prev - m_new)
        kw = w_col * k
        ct_ref[hd] = decay * ct + lax.dot_general(
            kw.astype(BF16), v, (((0,), (0,)), ((), ())), preferred_element_type=F32)
        n_ref[hd] = decay * n_row + jnp.sum(kw, axis=0, keepdims=True)
        m_ref[hd] = jnp.broadcast_to(m_new, (1, GATE_LANES))

        hm = _ln(hh, w["m_norm_g"][:, v_cols])
        ya_ref[rows, v_cols] = (og_ref[rows, v_cols] * hm).astype(BF16)

    def gmlp_group(grp):
        w_tri = jnp.where(causal, w["w_s"][grp], 0.0).astype(BF16)
        bias_col = w["b_s_t"][:, grp:grp + 1]
        cols = slice(grp * G_DG, (grp + 1) * G_DG)
        u_ref[:, cols] = jax.nn.gelu(act_r["u"][:, cols])
        for c in range(nch):
            rows = slice(c * CHUNK, (c + 1) * CHUNK)
            sg = _dot(w_tri, vln_ref[rows, cols]) + bias_col
            hg_ref[rows, cols] = (u_ref[rows, cols] * sg).astype(BF16)

    def wp(name):
        lo, hi = _P_COLS[name]
        return w["p"][:, lo:hi]

    def proj_block(name, lo, hi):
        base = _H_COLS[name][0]
        z = _dot(h_cur_ref[...], w["h"][:, base + lo:base + hi])
        act_w[name][:, lo:hi] = z.astype(act_w[name].dtype)

    def gates():
        pre = act_r["if"][...] + w["if_bias"][...]
        col = lax.broadcasted_iota(jnp.int32, (tm, GATE_LANES), 1)
        row_in_chunk = lax.broadcasted_iota(jnp.int32, (tm, GATE_LANES), 0) % CHUNK
        gates_ref[...] = jnp.where(
            col >= M_HEADS, _chunk_cumsum(_log_sigmoid(pre), row_in_chunk), pre)

    def conv_block(lo, hi):
        zqk = act_r["qk"][:, lo:hi]
        row_in_vreg = lax.broadcasted_iota(jnp.int32, (SUBLANES, hi - lo), 0)
        tail = tail_ref[:, lo:hi]
        conv = w["conv_b"][:, lo:hi] + w["conv_w"][CONV_W - 1:CONV_W, lo:hi] * zqk
        for s in range(1, CONV_W):
            j = CONV_W - 1 - s
            conv = conv + w["conv_w"][j:j + 1, lo:hi] * _shift_rows(zqk, tail, s, row_in_vreg)
        tail_ref[:, lo:hi] = zqk[tm - SUBLANES:tm, :]
        y = jax.nn.silu(conv)
        if lo < M_QK:
            q_ref[:, lo:hi] = y * (M_DQK ** -0.5)
        else:
            k_ref[:, lo - M_QK:hi - M_QK] = y

    def out_gate(hd):
        v_cols = slice(hd * M_DV, (hd + 1) * M_DV)
        og_ref[:, v_cols] = jax.nn.sigmoid(act_r["o"][:, v_cols])

    def gmlp_norm():
        vln_ref[...] = (_ln(jax.nn.gelu(act_r["vg"][...]), w["g_norm_g"][...])
                        + w["g_norm_b"][...]).astype(BF16)

    branch_b = {}

    def gmlp_out():
        branch_b["y"] = _dot(hg_ref[...], wp("p_b"))

    proj = []
    for name, width, _ in _ACT_SHAPES:
        for lo in range(0, width, MXU_COLS):
            hi = min(lo + MXU_COLS, width)
            proj.append((hi - lo, functools.partial(proj_block, name, lo, hi)))
    rec = [(300, gates)]
    rec += [(350, functools.partial(conv_block, lo, lo + MXU_COLS))
            for lo in range(0, 2 * M_QK, MXU_COLS)]
    rec += [(100, functools.partial(out_gate, hd)) for hd in range(M_HEADS)]
    rec += [(450, functools.partial(mlstm_unit, 0, hd)) for hd in range(M_HEADS)]
    rec += [(900, gmlp_norm)]
    late = [(450, functools.partial(mlstm_unit, c, hd))
            for c in range(1, nch) for hd in range(M_HEADS)]
    rec += late[:len(late) // 2]
    rec += [(300, functools.partial(gmlp_group, grp)) for grp in range(G_GROUPS)]
    rec += [(0, gmlp_out)]
    rec += late[len(late) // 2:]
    proj_total = sum(c for c, _ in proj)
    rec_total = sum(c for c, _ in rec)
    done_proj, done_rec, ip = 0, 0, 0
    for cost, run in rec:
        while ip < len(proj) and done_proj * rec_total <= done_rec * proj_total:
            done_proj += proj[ip][0]
            proj[ip][1]()
            ip += 1
        run()
        done_rec += cost
    for _, run in proj[ip:]:
        run()

    h_next_ref[...] = _rms(xn_ref[...], gains_ref[2:3, :]).astype(BF16)
    y_a = _dot(ya_ref[...], wp("p_a"))
    merged = (jax.nn.sigmoid(act_r["ga"][...]) * y_a
              + jax.nn.sigmoid(act_r["gb"][...]) * branch_b["y"])
    mix = _dot(merged.astype(BF16), wp("w_out"))
    o_ref[...] = xb_ref[...] + _rms(mix, gains_ref[3:4, :])


def _mixer_kernel(*refs, tiles_per_row):
    xn_ref, xb_ref, gains_ref = refs[:3]
    nw = len(_MIX_WEIGHTS)
    w = dict(zip(_MIX_WEIGHTS, refs[3:3 + nw]))
    o_ref = refs[3 + nw]
    scratch = refs[4 + nw:]
    act = [dict(zip((n for n, _, _ in _ACT_SHAPES), scratch[i * _N_ACT:(i + 1) * _N_ACT]))
           for i in range(2)]
    h_refs = scratch[2 * _N_ACT:2 * _N_ACT + 2]
    rest = scratch[2 * _N_ACT + 2:]
    tail_ref, ct_ref, n_ref, m_ref = rest[:4]
    t = pl.program_id(0)

    @pl.when(t == 0)
    def _():
        for ref in act[1].values():
            ref[...] = jnp.zeros_like(ref)
        h_refs[0][...] = _rms(xb_ref[...], gains_ref[2:3, :]).astype(BF16)

    @pl.when(jnp.logical_or(t == 0, (t + tiles_per_row - 1) % tiles_per_row == 0))
    def _():
        tail_ref[...] = jnp.zeros_like(tail_ref)
        ct_ref[...] = jnp.zeros_like(ct_ref)
        n_ref[...] = jnp.zeros_like(n_ref)
        m_ref[...] = jnp.zeros_like(m_ref)

    step = functools.partial(_mixer_step, xn_ref, xb_ref, gains_ref, w, o_ref)

    @pl.when(t % 2 == 0)
    def _():
        step(act[0], act[1], h_refs[0], h_refs[1], *rest)

    @pl.when(t % 2 == 1)
    def _():
        step(act[1], act[0], h_refs[1], h_refs[0], *rest)


def _mixer(x2d, gains, w, layer, tiles_per_row):
    tm = MIX_TM
    n_tiles = x2d.shape[0] // tm

    def lay(shape):
        nd = len(shape)
        return _resident((None,) + shape, lambda t: (layer,) + (0,) * nd)

    w_specs = {
        "h": lay((D_MODEL, _H_WIDTH)), "p": lay((D_MODEL, _P_WIDTH)),
        "conv_w": lay((CONV_W, 2 * M_QK)), "conv_b": lay((1, 2 * M_QK)),
        "if_bias": lay((1, GATE_LANES)), "m_norm_g": lay((1, M_V)),
        "g_norm_g": lay((1, G_WIDTH)), "g_norm_b": lay((1, G_WIDTH)),
        "w_s": lay((G_GROUPS, CHUNK, CHUNK)), "b_s_t": lay((CHUNK, G_GROUPS)),
    }
    in_specs = [
        pl.BlockSpec((tm, D_MODEL), lambda t: (jnp.minimum(t + 1, n_tiles - 1), 0)),
        pl.BlockSpec((tm, D_MODEL), lambda t: (jnp.maximum(t - 1, 0), 0)),
        lay((6, D_MODEL)),
    ] + [w_specs[name] for name in _MIX_WEIGHTS]
    act_scratch = [pltpu.VMEM((tm, width), dt) for _, width, dt in _ACT_SHAPES]
    return pl.pallas_call(
        functools.partial(_mixer_kernel, tiles_per_row=tiles_per_row),
        grid=(n_tiles + 1,),
        in_specs=in_specs,
        out_specs=pl.BlockSpec((tm, D_MODEL), lambda t: (jnp.maximum(t - 1, 0), 0)),
        out_shape=jax.ShapeDtypeStruct(x2d.shape, F32),
        scratch_shapes=act_scratch + act_scratch + [
            pltpu.VMEM((tm, D_MODEL), BF16),
            pltpu.VMEM((tm, D_MODEL), BF16),
            pltpu.VMEM((SUBLANES, 2 * M_QK), F32),
            pltpu.VMEM((M_HEADS, M_DQK, M_DV), F32),
            pltpu.VMEM((M_HEADS, 1, M_DQK), F32),
            pltpu.VMEM((M_HEADS, 1, GATE_LANES), F32),
            pltpu.VMEM((tm, M_V), BF16),
            pltpu.VMEM((tm, G_WIDTH), BF16),
            pltpu.VMEM((tm, M_QK), F32),
            pltpu.VMEM((tm, M_QK), F32),
            pltpu.VMEM((tm, GATE_LANES), F32),
            pltpu.VMEM((tm, M_V), F32),
            pltpu.VMEM((tm, G_WIDTH), F32),
            pltpu.VMEM((tm, G_WIDTH), BF16),
        ],
        compiler_params=pltpu.CompilerParams(
            dimension_semantics=("arbitrary",), vmem_limit_bytes=VMEM_LIMIT_BYTES),
        name="mixer",
    )(x2d, x2d, gains, *[w[name] for name in _MIX_WEIGHTS])


PACK_ROWS = 256


def _pack_w_in_kernel(w_ref, tail_ref, o_ref):
    gate_lo = 2 * M_QK + M_V
    n_gate = 2 * M_HEADS
    o_ref[:, :gate_lo] = w_ref[:, :gate_lo]
    lane = lax.broadcasted_iota(jnp.int32, (PACK_ROWS, GATE_LANES), 1)
    o_ref[:, gate_lo:gate_lo + GATE_LANES] = jnp.where(
        lane < n_gate, w_ref[:, gate_lo:gate_lo + GATE_LANES], jnp.zeros((), BF16))
    o_ref[:, gate_lo + GATE_LANES:] = jnp.concatenate(
        [w_ref[:, gate_lo:][:, n_gate:], tail_ref[...]], axis=1)


def _pack_w_in(w_in):
    depth, rows, cols = w_in.shape
    aligned = (cols // LANES) * LANES
    return pl.pallas_call(
        _pack_w_in_kernel,
        grid=(depth, rows // PACK_ROWS),
        in_specs=[pl.BlockSpec((None, PACK_ROWS, aligned), lambda l, i: (l, i, 0)),
                  pl.BlockSpec((None, PACK_ROWS, cols - aligned), lambda l, i: (l, i, 0))],
        out_specs=pl.BlockSpec((None, PACK_ROWS, _H_WIDTH), lambda l, i: (l, i, 0)),
        out_shape=jax.ShapeDtypeStruct((depth, rows, _H_WIDTH), BF16),
        compiler_params=pltpu.CompilerParams(
            dimension_semantics=("arbitrary", "arbitrary"), vmem_limit_bytes=VMEM_LIMIT_BYTES),
        name="pack_w_in",
    )(w_in[:, :, :aligned].astype(BF16), w_in[:, :, aligned:].astype(BF16))


def _pad_cols(w):
    return jnp.pad(w, ((0, 0), (0, 0), (0, PAD_COLS))).astype(BF16)


def kernel(x, norm_gains, ffn1_w_in, ffn1_w_out, w_in, conv_w, conv_b, i_bias, f_bias, m_norm_g, g_norm_g, g_norm_b, w_s, b_s, p_a, p_b, w_out, ffn2_w_in, ffn2_w_out):
    b, s, d = x.shape
    w = dict(
        h=_pack_w_in(w_in),
        p=_pad_cols(jnp.concatenate([p_a, p_b, w_out], axis=-1)),
        conv_w=conv_w,
        conv_b=conv_b[:, None, :],
        if_bias=jnp.pad(jnp.concatenate([i_bias, f_bias], axis=-1),
                        ((0, 0), (0, GATE_LANES - 2 * M_HEADS)))[:, None, :],
        m_norm_g=m_norm_g[:, None, :],
        g_norm_g=g_norm_g[:, None, :],
        g_norm_b=g_norm_b[:, None, :],
        w_s=w_s,
        b_s_t=jnp.swapaxes(b_s, 1, 2),
    )
    ffn_w = [(ffn1_w_in, ffn1_w_out), (ffn2_w_in, ffn2_w_out)]
    w_in_bf, w_out_bf = ffn1_w_in[0].astype(BF16), ffn1_w_out[0].astype(BF16)

    x2d = x.reshape(b * s, d)
    for layer in range(DEPTH):
        for which, (gi, go) in enumerate(((0, 1), (4, 5))):
            call = 2 * layer + which
            nxt = None
            if call + 1 < 2 * DEPTH:
                nxt = ffn_w[(call + 1) % 2] + ((call + 1) // 2,)
            x2d, w_in_bf, w_out_bf = _ffn(x2d, norm_gains, w_in_bf, w_out_bf, layer, gi, go, nxt)
            if which == 0:
                x2d = _mixer(x2d, norm_gains, w, layer, s // MIX_TM)
    return x2d.reshape(b, s, d)
```

```python
import functools

import jax
import jax.numpy as jnp
from jax import lax
from jax.experimental import pallas as pl
from jax.experimental.pallas import tpu as pltpu

D_MODEL = 1024
DEPTH = 4
M_HEADS = 4
M_DQK = 128
M_DV = 256
M_QK = M_HEADS * M_DQK
M_V = M_HEADS * M_DV
CHUNK = 128
CONV_W = 4
G_GROUPS = 4
G_WIDTH = 1024
G_DG = G_WIDTH // G_GROUPS
D_FF = 2816
EPS = 1e-6
GATE_LANES = 128
SUBLANES = 8
MXU_COLS = 256

FFN_TM = 512
FFN_ROWS = 256
FFN_CAST_ROWS = 128
MIX_TM = 256
VMEM_LIMIT_BYTES = 56 * 1024 * 1024

BF16 = jnp.bfloat16
F32 = jnp.float32


def _dot(a, b):
    return jnp.dot(a, b, preferred_element_type=F32)


def _rms(x, g):
    return x * lax.rsqrt(jnp.mean(x * x, axis=-1, keepdims=True) + EPS) * g


def _ln(x, g):
    mu = jnp.mean(x, axis=-1, keepdims=True)
    xc = x - mu
    var = jnp.mean(xc * xc, axis=-1, keepdims=True)
    return xc * lax.rsqrt(var + EPS) * g


def _log_sigmoid(x):
    return jnp.minimum(x, 0.0) - jnp.log1p(jnp.exp(-jnp.abs(x)))


def _cast_stream(t, src_hbm, dst_hbm, stage_f32, stage_bf16, sem_in, sem_out, first_step):
    chunk = stage_f32.shape[1]
    n = src_hbm.shape[0] // chunk
    k = t - first_step

    def rows(kk):
        return pl.ds(pl.multiple_of(kk * chunk, chunk), chunk)

    def fetch(kk):
        return pltpu.make_async_copy(src_hbm.at[rows(kk), :], stage_f32.at[kk % 2],
                                     sem_in.at[kk % 2])

    def write_back(kk):
        return pltpu.make_async_copy(stage_bf16.at[kk % 2], dst_hbm.at[rows(kk), :],
                                     sem_out.at[kk % 2])

    @pl.when(k == 0)
    def _():
        fetch(k).start()

    @pl.when(jnp.logical_and(k >= 0, k < n))
    def _():
        @pl.when(k + 1 < n)
        def _():
            fetch(k + 1).start()

        fetch(k).wait()

        @pl.when(k >= 2)
        def _():
            write_back(k - 2).wait()

        stage_bf16[k % 2] = stage_f32[k % 2].astype(BF16)
        write_back(k).start()

    @pl.when(k == n - 1)
    def _():
        write_back(k - 1).wait()
        write_back(k).wait()


def _ffn_body(x_ref, gains_ref, w_in_ref, w_out_ref, o_ref, gi, go):
    for r in range(0, FFN_TM, FFN_ROWS):
        rows = slice(r, r + FFN_ROWS)
        x = x_ref[rows, :]
        h = _rms(x, gains_ref[gi:gi + 1, :]).astype(BF16)
        a = _dot(h, w_in_ref[:, :D_FF])
        g = _dot(h, w_in_ref[:, D_FF:])
        s = (jax.nn.silu(a) * g).astype(BF16)
        y = _dot(s, w_out_ref[...])
        o_ref[rows, :] = x + 0.5 * _rms(y, gains_ref[go:go + 1, :])


def _ffn_kernel(x_ref, gains_ref, w_in_ref, w_out_ref, o_ref, *, gi, go):
    _ffn_body(x_ref, gains_ref, w_in_ref, w_out_ref, o_ref, gi, go)


def _ffn_prep_kernel(x_ref, gains_ref, w_in_ref, w_out_ref, nxt_in_hbm, nxt_out_hbm,
                     o_ref, cast_in_hbm, cast_out_hbm,
                     in_f32, in_bf16, out_f32, out_bf16, sem_a, sem_b, sem_c, sem_d,
                     *, gi, go, nxt_layer):
    _ffn_body(x_ref, gains_ref, w_in_ref, w_out_ref, o_ref, gi, go)
    t = pl.program_id(0)
    n_in = D_MODEL // FFN_CAST_ROWS
    _cast_stream(t, nxt_in_hbm.at[nxt_layer], cast_in_hbm, in_f32, in_bf16,
                 sem_a, sem_b, 0)
    _cast_stream(t, nxt_out_hbm.at[nxt_layer], cast_out_hbm, out_f32, out_bf16,
                 sem_c, sem_d, n_in)


def _resident(shape, index_map):
    return pl.BlockSpec(shape, index_map, pipeline_mode=pl.Buffered(1))


def _ffn(x2d, gains, w_in, w_out, layer, gi, go, nxt=None):
    n_steps = x2d.shape[0] // FFN_TM
    in_specs = [
        pl.BlockSpec((FFN_TM, D_MODEL), lambda i: (i, 0)),
        _resident((None, 6, D_MODEL), lambda i: (layer, 0, 0)),
        _resident((D_MODEL, 2 * D_FF), lambda i: (0, 0)),
        _resident((D_FF, D_MODEL), lambda i: (0, 0)),
    ]
    x_spec = pl.BlockSpec((FFN_TM, D_MODEL), lambda i: (i, 0))
    x_shape = jax.ShapeDtypeStruct(x2d.shape, F32)
    params = pltpu.CompilerParams(
        dimension_semantics=("arbitrary",), vmem_limit_bytes=VMEM_LIMIT_BYTES)
    if nxt is None:
        out = pl.pallas_call(
            functools.partial(_ffn_kernel, gi=gi, go=go),
            grid=(n_steps,), in_specs=in_specs, out_specs=x_spec, out_shape=x_shape,
            compiler_params=params, name="ffn",
        )(x2d, gains, w_in, w_out)
        return out, None, None
    nxt_in, nxt_out, nxt_layer = nxt
    assert (D_MODEL + D_FF) // FFN_CAST_ROWS <= n_steps
    hbm = pl.BlockSpec(memory_space=pl.ANY)
    return pl.pallas_call(
        functools.partial(_ffn_prep_kernel, gi=gi, go=go, nxt_layer=nxt_layer),
        grid=(n_steps,),
        in_specs=in_specs + [hbm, hbm],
        out_specs=[x_spec, hbm, hbm],
        out_shape=[x_shape, jax.ShapeDtypeStruct((D_MODEL, 2 * D_FF), BF16),
                   jax.ShapeDtypeStruct((D_FF, D_MODEL), BF16)],
        scratch_shapes=[
            pltpu.VMEM((2, FFN_CAST_ROWS, 2 * D_FF), F32),
            pltpu.VMEM((2, FFN_CAST_ROWS, 2 * D_FF), BF16),
            pltpu.VMEM((2, FFN_CAST_ROWS, D_MODEL), F32),
            pltpu.VMEM((2, FFN_CAST_ROWS, D_MODEL), BF16),
            pltpu.SemaphoreType.DMA((2,)), pltpu.SemaphoreType.DMA((2,)),
            pltpu.SemaphoreType.DMA((2,)), pltpu.SemaphoreType.DMA((2,)),
        ],
        compiler_params=params, name="ffn_prep",
    )(x2d, gains, w_in, w_out, nxt_in, nxt_out)


_ACT_SHAPES = (
    ("qk", 2 * M_QK, F32), ("if", GATE_LANES, F32), ("v", M_V, BF16), ("o", M_V, F32),
    ("u", G_WIDTH, F32), ("vg", G_WIDTH, F32), ("ga", D_MODEL, F32), ("gb", D_MODEL, F32),
)
_N_ACT = len(_ACT_SHAPES)
_MIX_WEIGHTS = ("h", "p", "conv_w", "conv_b", "if_bias",
                "m_norm_g", "g_norm_g", "g_norm_b", "w_s", "b_s_t")

LANES = 128
PAD_COLS = LANES


def _col_ranges(names_widths):
    out, off = {}, 0
    for name, width in names_widths:
        out[name] = (off, off + width)
        off += width
    return out, off


_H_COLS, _H_WIDTH = _col_ranges((("qk", 2 * M_QK), ("v", M_V), ("if", GATE_LANES), ("o", M_V),
                                 ("u", G_WIDTH), ("vg", G_WIDTH), ("ga", D_MODEL), ("gb", D_MODEL)))
_P_COLS, _P_WIDTH = _col_ranges((("p_a", D_MODEL), ("p_b", D_MODEL), ("w_out", D_MODEL),
                                 ("pad", PAD_COLS)))


def _chunk_cumsum(v, row_in_chunk):
    shift = 1
    while shift < CHUNK:
        rolled = pltpu.roll(v, shift, axis=0)
        v = v + jnp.where(row_in_chunk >= shift, rolled, 0.0)
        shift *= 2
    return v


def _shift_rows(cur, tail, s, row_in_vreg):
    rolled = pltpu.roll(cur, s, axis=0)
    head = jnp.where(row_in_vreg < s, pltpu.roll(tail, s, axis=0), rolled[0:SUBLANES])
    return jnp.concatenate([head, rolled[SUBLANES:]], axis=0)


def _mixer_step(xn_ref, xb_ref, gains_ref, w, o_ref, act_w, act_r, h_cur_ref, h_next_ref,
                tail_ref, ct_ref, n_ref, m_ref, ya_ref, hg_ref, q_ref, k_ref, gates_ref,
                og_ref, u_ref, vln_ref):
    tm = MIX_TM
    nch = tm // CHUNK

    tri_r = lax.broadcasted_iota(jnp.int32, (CHUNK, CHUNK), 0)
    tri_c = lax.broadcasted_iota(jnp.int32, (CHUNK, CHUNK), 1)
    causal = tri_r >= tri_c

    def mlstm_unit(c, hd):
        rows = slice(c * CHUNK, (c + 1) * CHUNK)
        qk_cols = slice(hd * M_DQK, (hd + 1) * M_DQK)
        v_cols = slice(hd * M_DV, (hd + 1) * M_DV)
        gc = gates_ref[rows, :]
        gct = gc.T
        q = q_ref[rows, qk_cols]
        k = k_ref[rows, qk_cols]
        v = act_r["v"][rows, v_cols]
        i_col = gc[:, hd:hd + 1]
        b_col = gc[:, M_HEADS + hd:M_HEADS + hd + 1]
        i_row = gct[hd:hd + 1, :]
        b_row = gct[M_HEADS + hd:M_HEADS + hd + 1, :]
        b_last = b_row[:, CHUNK - 1:CHUNK]
        m_prev = m_ref[hd][:, 0:1]
        ct = ct_ref[hd]
        n_row = n_ref[hd]

        a_row = i_row - b_row
        d = jnp.where(causal, b_col + a_row, -jnp.inf)
        inter = b_col + m_prev
        m_t = jnp.maximum(inter, jnp.max(d, axis=-1, keepdims=True))
        s = lax.dot_general(q.astype(BF16), k.astype(BF16), (((1,), (1,)), ((), ())),
                            preferred_element_type=F32)
        p = jnp.exp(d - m_t) * s
        s_inter = jnp.exp(inter - m_t)
        lhs = jnp.concatenate([p, s_inter * q], axis=1).astype(BF16)
        rhs = jnp.concatenate([v, ct.astype(BF16)], axis=0)
        num = _dot(lhs, rhs)
        den = (jnp.sum(p, axis=-1, keepdims=True)
               + s_inter * jnp.sum(q * n_row, axis=-1, keepdims=True))
        hh = num / jnp.maximum(jnp.abs(den), jnp.exp(-m_t))

        g_row = b_last + a_row
        m_new = jnp.maximum(b_last + m_prev, jnp.max(g_row, axis=-1, keepdims=True))
        w_col = jnp.exp(b_last - b_col + i_col - m_new)
        decay = jnp.exp(b_last + m_prev - m_new)
        kw = w_col * k
        ct_ref[hd] = decay * ct + lax.dot_general(
            kw.astype(BF16), v, (((0,), (0,)), ((), ())), preferred_element_type=F32)
        n_ref[hd] = decay * n_row + jnp.sum(kw, axis=0, keepdims=True)
        m_ref[hd] = jnp.broadcast_to(m_new, (1, GATE_LANES))

        hm = _ln(hh, w["m_norm_g"][:, v_cols])
        ya_ref[rows, v_cols] = (og_ref[rows, v_cols] * hm).astype(BF16)

    def gmlp_group(grp):
        w_tri = jnp.where(causal, w["w_s"][grp], 0.0).astype(BF16)
        bias_col = w["b_s_t"][:, grp:grp + 1]
        cols = slice(grp * G_DG, (grp + 1) * G_DG)
        u_ref[:, cols] = jax.nn.gelu(act_r["u"][:, cols])
        for c in range(nch):
            rows = slice(c * CHUNK, (c + 1) * CHUNK)
            sg = _dot(w_tri, vln_ref[rows, cols]) + bias_col
            hg_ref[rows, cols] = (u_ref[rows, cols] * sg).astype(BF16)

    def wp(name):
        lo, hi = _P_COLS[name]
        return w["p"][:, lo:hi]

    def proj_block(name, lo, hi):
        base = _H_COLS[name][0]
        z = _dot(h_cur_ref[...], w["h"][:, base + lo:base + hi])
        act_w[name][:, lo:hi] = z.astype(act_w[name].dtype)

    def gates():
        pre = act_r["if"][...] + w["if_bias"][...]
        col = lax.broadcasted_iota(jnp.int32, (tm, GATE_LANES), 1)
        row_in_chunk = lax.broadcasted_iota(jnp.int32, (tm, GATE_LANES), 0) % CHUNK
        gates_ref[...] = jnp.where(
            col >= M_HEADS, _chunk_cumsum(_log_sigmoid(pre), row_in_chunk), pre)

    def conv_block(lo, hi):
        zqk = act_r["qk"][:, lo:hi]
        row_in_vreg = lax.broadcasted_iota(jnp.int32, (SUBLANES, hi - lo), 0)
        tail = tail_ref[:, lo:hi]
        conv = w["conv_b"][:, lo:hi] + w["conv_w"][CONV_W - 1:CONV_W, lo:hi] * zqk
        for s in range(1, CONV_W):
            j = CONV_W - 1 - s
            conv = conv + w["conv_w"][j:j + 1, lo:hi] * _shift_rows(zqk, tail, s, row_in_vreg)
        tail_ref[:, lo:hi] = zqk[tm - SUBLANES:tm, :]
        y = jax.nn.silu(conv)
        if lo < M_QK:
            q_ref[:, lo:hi] = y * (M_DQK ** -0.5)
        else:
            k_ref[:, lo - M_QK:hi - M_QK] = y

    def out_gate(hd):
        v_cols = slice(hd * M_DV, (hd + 1) * M_DV)
        og_ref[:, v_cols] = jax.nn.sigmoid(act_r["o"][:, v_cols])

    def gmlp_norm():
        vln_ref[...] = (_ln(jax.nn.gelu(act_r["vg"][...]), w["g_norm_g"][...])
                        + w["g_norm_b"][...]).astype(BF16)

    branch_b = {}

    def gmlp_out():
        branch_b["y"] = _dot(hg_ref[...], wp("p_b"))

    proj = []
    for name, width, _ in _ACT_SHAPES:
        for lo in range(0, width, MXU_COLS):
            hi = min(lo + MXU_COLS, width)
            proj.append((hi - lo, functools.partial(proj_block, name, lo, hi)))
    rec = [(300, gates)]
    rec += [(350, functools.partial(conv_block, lo, lo + MXU_COLS))
            for lo in range(0, 2 * M_QK, MXU_COLS)]
    rec += [(100, functools.partial(out_gate, hd)) for hd in range(M_HEADS)]
    rec += [(450, functools.partial(mlstm_unit, 0, hd)) for hd in range(M_HEADS)]
    rec += [(900, gmlp_norm)]
    late = [(450, functools.partial(mlstm_unit, c, hd))
            for c in range(1, nch) for hd in range(M_HEADS)]
    rec += late[:len(late) // 2]
    rec += [(300, functools.partial(gmlp_group, grp)) for grp in range(G_GROUPS)]
    rec += [(0, gmlp_out)]
    rec += late[len(late) // 2:]
    proj_total = sum(c for c, _ in proj)
    rec_total = sum(c for c, _ in rec)
    done_proj, done_rec, ip = 0, 0, 0
    for cost, run in rec:
        while ip < len(proj) and done_proj * rec_total <= done_rec * proj_total:
            done_proj += proj[ip][0]
            proj[ip][1]()
            ip += 1
        run()
        done_rec += cost
    for _, run in proj[ip:]:
        run()

    h_next_ref[...] = _rms(xn_ref[...], gains_ref[2:3, :]).astype(BF16)
    y_a = _dot(ya_ref[...], wp("p_a"))
    merged = (jax.nn.sigmoid(act_r["ga"][...]) * y_a
              + jax.nn.sigmoid(act_r["gb"][...]) * branch_b["y"])
    mix = _dot(merged.astype(BF16), wp("w_out"))
    o_ref[...] = xb_ref[...] + _rms(mix, gains_ref[3:4, :])


def _mixer_kernel(*refs, tiles_per_row):
    xn_ref, xb_ref, gains_ref = refs[:3]
    nw = len(_MIX_WEIGHTS)
    w = dict(zip(_MIX_WEIGHTS, refs[3:3 + nw]))
    o_ref = refs[3 + nw]
    scratch = refs[4 + nw:]
    act = [dict(zip((n for n, _, _ in _ACT_SHAPES), scratch[i * _N_ACT:(i + 1) * _N_ACT]))
           for i in range(2)]
    h_refs = scratch[2 * _N_ACT:2 * _N_ACT + 2]
    rest = scratch[2 * _N_ACT + 2:]
    tail_ref, ct_ref, n_ref, m_ref = rest[:4]
    t = pl.program_id(0)

    @pl.when(t == 0)
    def _():
        for ref in act[1].values():
            ref[...] = jnp.zeros_like(ref)
        h_refs[0][...] = _rms(xb_ref[...], gains_ref[2:3, :]).astype(BF16)

    @pl.when(jnp.logical_or(t == 0, (t + tiles_per_row - 1) % tiles_per_row == 0))
    def _():
        tail_ref[...] = jnp.zeros_like(tail_ref)
        ct_ref[...] = jnp.zeros_like(ct_ref)
        n_ref[...] = jnp.zeros_like(n_ref)
        m_ref[...] = jnp.zeros_like(m_ref)

    step = functools.partial(_mixer_step, xn_ref, xb_ref, gains_ref, w, o_ref)

    @pl.when(t % 2 == 0)
    def _():
        step(act[0], act[1], h_refs[0], h_refs[1], *rest)

    @pl.when(t % 2 == 1)
    def _():
        step(act[1], act[0], h_refs[1], h_refs[0], *rest)


def _mixer(x2d, gains, w, layer, tiles_per_row):
    tm = MIX_TM
    n_tiles = x2d.shape[0] // tm

    def lay(shape):
        nd = len(shape)
        return _resident((None,) + shape, lambda t: (layer,) + (0,) * nd)

    w_specs = {
        "h": lay((D_MODEL, _H_WIDTH)), "p": lay((D_MODEL, _P_WIDTH)),
        "conv_w": lay((CONV_W, 2 * M_QK)), "conv_b": lay((1, 2 * M_QK)),
        "if_bias": lay((1, GATE_LANES)), "m_norm_g": lay((1, M_V)),
        "g_norm_g": lay((1, G_WIDTH)), "g_norm_b": lay((1, G_WIDTH)),
        "w_s": lay((G_GROUPS, CHUNK, CHUNK)), "b_s_t": lay((CHUNK, G_GROUPS)),
    }
    in_specs = [
        pl.BlockSpec((tm, D_MODEL), lambda t: (jnp.minimum(t + 1, n_tiles - 1), 0)),
        pl.BlockSpec((tm, D_MODEL), lambda t: (jnp.maximum(t - 1, 0), 0)),
        lay((6, D_MODEL)),
    ] + [w_specs[name] for name in _MIX_WEIGHTS]
    act_scratch = [pltpu.VMEM((tm, width), dt) for _, width, dt in _ACT_SHAPES]
    return pl.pallas_call(
        functools.partial(_mixer_kernel, tiles_per_row=tiles_per_row),
        grid=(n_tiles + 1,),
        in_specs=in_specs,
        out_specs=pl.BlockSpec((tm, D_MODEL), lambda t: (jnp.maximum(t - 1, 0), 0)),
        out_shape=jax.ShapeDtypeStruct(x2d.shape, F32),
        scratch_shapes=act_scratch + act_scratch + [
            pltpu.VMEM((tm, D_MODEL), BF16),
            pltpu.VMEM((tm, D_MODEL), BF16),
            pltpu.VMEM((SUBLANES, 2 * M_QK), F32),
            pltpu.VMEM((M_HEADS, M_DQK, M_DV), F32),
            pltpu.VMEM((M_HEADS, 1, M_DQK), F32),
            pltpu.VMEM((M_HEADS, 1, GATE_LANES), F32),
            pltpu.VMEM((tm, M_V), BF16),
            pltpu.VMEM((tm, G_WIDTH), BF16),
            pltpu.VMEM((tm, M_QK), F32),
            pltpu.VMEM((tm, M_QK), F32),
            pltpu.VMEM((tm, GATE_LANES), F32),
            pltpu.VMEM((tm, M_V), F32),
            pltpu.VMEM((tm, G_WIDTH), F32),
            pltpu.VMEM((tm, G_WIDTH), BF16),
        ],
        compiler_params=pltpu.CompilerParams(
            dimension_semantics=("arbitrary",), vmem_limit_bytes=VMEM_LIMIT_BYTES),
        name="mixer",
    )(x2d, x2d, gains, *[w[name] for name in _MIX_WEIGHTS])


PACK_ROWS = 256


def _pack_w_in_kernel(w_ref, tail_ref, o_ref):
    gate_lo = 2 * M_QK + M_V
    n_gate = 2 * M_HEADS
    o_ref[:, :gate_lo] = w_ref[:, :gate_lo]
    lane = lax.broadcasted_iota(jnp.int32, (PACK_ROWS, GATE_LANES), 1)
    o_ref[:, gate_lo:gate_lo + GATE_LANES] = jnp.where(
        lane < n_gate, w_ref[:, gate_lo:gate_lo + GATE_LANES], jnp.zeros((), BF16))
    o_ref[:, gate_lo + GATE_LANES:] = jnp.concatenate(
        [w_ref[:, gate_lo:][:, n_gate:], tail_ref[...]], axis=1)


def _pack_w_in(w_in):
    depth, rows, cols = w_in.shape
    aligned = (cols // LANES) * LANES
    return pl.pallas_call(
        _pack_w_in_kernel,
        grid=(depth, rows // PACK_ROWS),
        in_specs=[pl.BlockSpec((None, PACK_ROWS, aligned), lambda l, i: (l, i, 0)),
                  pl.BlockSpec((None, PACK_ROWS, cols - aligned), lambda l, i: (l, i, 0))],
        out_specs=pl.BlockSpec((None, PACK_ROWS, _H_WIDTH), lambda l, i: (l, i, 0)),
        out_shape=jax.ShapeDtypeStruct((depth, rows, _H_WIDTH), BF16),
        compiler_params=pltpu.CompilerParams(
            dimension_semantics=("arbitrary", "arbitrary"), vmem_limit_bytes=VMEM_LIMIT_BYTES,
            allow_input_fusion=[True, True]),
        name="pack_w_in",
    )(w_in[:, :, :aligned].astype(BF16), w_in[:, :, aligned:].astype(BF16))


def _pad_cols(w):
    return jnp.pad(w, ((0, 0), (0, 0), (0, PAD_COLS))).astype(BF16)


def kernel(x, norm_gains, ffn1_w_in, ffn1_w_out, w_in, conv_w, conv_b, i_bias, f_bias, m_norm_g, g_norm_g, g_norm_b, w_s, b_s, p_a, p_b, w_out, ffn2_w_in, ffn2_w_out):
    b, s, d = x.shape
    w = dict(
        h=_pack_w_in(w_in),
        p=_pad_cols(jnp.concatenate([p_a, p_b, w_out], axis=-1)),
        conv_w=conv_w,
        conv_b=conv_b[:, None, :],
        if_bias=jnp.pad(jnp.concatenate([i_bias, f_bias], axis=-1),
                        ((0, 0), (0, GATE_LANES - 2 * M_HEADS)))[:, None, :],
        m_norm_g=m_norm_g[:, None, :],
        g_norm_g=g_norm_g[:, None, :],
        g_norm_b=g_norm_b[:, None, :],
        w_s=w_s,
        b_s_t=jnp.swapaxes(b_s, 1, 2),
    )
    ffn_w = [(ffn1_w_in, ffn1_w_out), (ffn2_w_in, ffn2_w_out)]
    w_in_bf, w_out_bf = ffn1_w_in[0].astype(BF16), ffn1_w_out[0].astype(BF16)

    x2d = x.reshape(b * s, d)
    for layer in range(DEPTH):
        for which, (gi, go) in enumerate(((0, 1), (4, 5))):
            call = 2 * layer + which
            nxt = None
            if call + 1 < 2 * DEPTH:
                nxt = ffn_w[(call + 1) % 2] + ((call + 1) // 2,)
            x2d, w_in_bf, w_out_bf = _ffn(x2d, norm_gains, w_in_bf, w_out_bf, layer, gi, go, nxt)
            if which == 0:
                x2d = _mixer(x2d, norm_gains, w, layer, s // MIX_TM)
    return x2d.reshape(b, s, d)
```
